```python
import jax, jax.numpy as jnp
from jax import lax
import numpy as np

D_MODEL = 1024
BATCH = 4
SEQ = 4096
DEPTH = 4

CHUNK = 64
N_MEM = 256
N_BRANCH = 4
MIX_W = D_MODEL // 2
CONV_WIDTH = 31
GLA_H = 4
GLA_DK = MIX_W // (2 * GLA_H)
GLA_DV = MIX_W // GLA_H
GLA_LOWRANK = 16
GLA_TAU = 16.0
SB_DH = 64
SB_H = MIX_W // SB_DH
SB_BLOCK = 128
LRU_BLOCKS = 8
LRU_BW = MIX_W // LRU_BLOCKS
LRU_CONV = 4
LRU_C = 8.0
MEM_H = 4
MEM_DH = D_MODEL // MEM_H
N_EXPERTS = 32
TOP_K = 4
MOE_FF = D_MODEL
SWIGLU_LIMIT = 7.0
SWIGLU_ALPHA = 1.702
MOE_BLOCK = 128
LN_EPS = 1e-5
DEEPNORM_ALPHA = (2 * DEPTH) ** 0.25
DEEPNORM_BETA = (8 * DEPTH) ** -0.25
IN_SIZES = (MIX_W, MIX_W,
            GLA_H * GLA_DK, GLA_H * GLA_DK, GLA_H * GLA_DV, GLA_H * GLA_DV, GLA_LOWRANK,
            SB_H * SB_DH, SB_H * SB_DH, SB_H * SB_DH,
            MIX_W, MIX_W,
            N_BRANCH * D_MODEL)
IN_COLS = sum(IN_SIZES)
SPLIT_POINTS = tuple(int(v) for v in np.cumsum(IN_SIZES)[:-1])

kernel_name = "hybrid_streaming_encoder_deepnorm_moe"


def layer_norm(x, g, b):
    xf = x.astype(jnp.float32)
    mu = jnp.mean(xf, axis=-1, keepdims=True)
    var = jnp.mean(jnp.square(xf - mu), axis=-1, keepdims=True)
    return ((xf - mu) * lax.rsqrt(var + LN_EPS) * g + b).astype(x.dtype)


def causal_depthwise_conv(u, w, b):
    width = w.shape[0]
    y = lax.conv_general_dilated(u, w[:, None, :], window_strides=(1,), padding=[(width - 1, 0)],
                                 dimension_numbers=("NWC", "WIO", "NWC"),
                                 feature_group_count=u.shape[-1])
    return y + b


def gla_chunked(q, k, v, log_a):
    B, S, H, DK = q.shape
    DV = v.shape[-1]
    nc = S // CHUNK
    f32 = jnp.float32
    qf = q.astype(f32).reshape(B, nc, CHUNK, H, DK)
    kf = k.astype(f32).reshape(B, nc, CHUNK, H, DK)
    vf = v.astype(f32).reshape(B, nc, CHUNK, H, DV)
    b = jnp.cumsum(log_a.astype(f32).reshape(B, nc, CHUNK, H, DK), axis=2)
    b_last = b[:, :, -1:]
    q_dec = qf * jnp.exp(b)
    k_inv = kf * jnp.exp(-b)
    k_end = kf * jnp.exp(b_last - b)
    causal = jnp.tril(jnp.ones((CHUNK, CHUNK), bool))
    s_intra = jnp.where(causal, jnp.einsum("bcthk,bcshk->bchts", q_dec, k_inv), 0.0)
    o_intra = jnp.einsum("bchts,bcshv->bcthv", s_intra, vf)
    kv = jnp.einsum("bcshk,bcshv->bchkv", k_end, vf)
    decay = jnp.exp(b_last[:, :, 0])

    def step(state, inp):
        dec_c, kv_c = inp
        return dec_c[..., None] * state + kv_c, state

    init = jnp.zeros((B, H, DK, DV), f32)
    _, s_prev = lax.scan(step, init, (jnp.moveaxis(decay, 1, 0), jnp.moveaxis(kv, 1, 0)))
    s_prev = jnp.moveaxis(s_prev, 0, 1)
    o_inter = jnp.einsum("bcthk,bchkv->bcthv", q_dec, s_prev)
    return (o_intra + o_inter).reshape(B, S, H, DV)


def stick_breaking(q, k, v):
    B, S, H, DH = q.shape
    scale = DH ** -0.5
    outs = []
    for blk in range(S // SB_BLOCK):
        q0 = blk * SB_BLOCK
        end = q0 + SB_BLOCK
        z = jnp.einsum("bthd,bshd->bhts", q[:, q0:end], k[:, :end]).astype(jnp.float32) * scale
        t_pos = q0 + jnp.arange(SB_BLOCK)
        s_pos = jnp.arange(end)
        before = s_pos[None, :] < t_pos[:, None]
        log_keep = jnp.where(before, jax.nn.log_sigmoid(-z), 0.0)
        log_tail = lax.cumsum(log_keep, axis=3, reverse=True) - log_keep
        w = jnp.where(before, jnp.exp(jax.nn.log_sigmoid(z) + log_tail), 0.0)
        outs.append(jnp.einsum("bhts,bshd->bthd", w.astype(v.dtype), v[:, :end]))
    return jnp.concatenate(outs, axis=1)


def rg_lru(xc, wa, ba, wx, bx, lam):
    B, S, C = xc.shape
    xb = xc.reshape(B, S, LRU_BLOCKS, LRU_BW)
    r = jax.nn.sigmoid(jnp.einsum("bsnc,ncd->bsnd", xb, wa).reshape(B, S, C) + ba)
    i = jax.nn.sigmoid(jnp.einsum("bsnc,ncd->bsnd", xb, wx).reshape(B, S, C) + bx)
    log_a = LRU_C * r.astype(jnp.float32) * jax.nn.log_sigmoid(lam.astype(jnp.float32))
    a = jnp.exp(log_a)
    u = jnp.sqrt(-jnp.expm1(2.0 * log_a)) * (i * xc).astype(jnp.float32)

    def combine(c1, c2):
        a1, b1 = c1
        a2, b2 = c2
        return a1 * a2, a2 * b1 + b2

    _, h = lax.associative_scan(combine, (a, u), axis=1)
    return h.astype(xc.dtype)


def hybrid_mixer(h, w_in, b_in, conv_a_w, conv_a_b, ln_a_g, ln_a_b, gla_wa2, gla_ba, gla_norm_g,
                 conv_d_w, conv_d_b, lru_wa, lru_ba, lru_wx, lru_bx, lru_lambda, w_branch, w_out, b_out):
    B, S, D = h.shape
    z = h @ w_in + b_in
    (a_val, a_gate, b_q, b_k, b_v, b_r, b_lr, c_q, c_k, c_v, d_x, d_g, g_merge) = jnp.split(
        z, SPLIT_POINTS, axis=-1)
    u = a_val * jax.nn.sigmoid(a_gate)
    y_a = jax.nn.silu(layer_norm(causal_depthwise_conv(u, conv_a_w, conv_a_b), ln_a_g, ln_a_b))
    log_a = jax.nn.log_sigmoid((b_lr @ gla_wa2 + gla_ba).astype(jnp.float32)) / GLA_TAU
    o_b = gla_chunked(b_q.reshape(B, S, GLA_H, GLA_DK) * (GLA_DK ** -0.5),
                      b_k.reshape(B, S, GLA_H, GLA_DK),
                      b_v.reshape(B, S, GLA_H, GLA_DV),
                      log_a.reshape(B, S, GLA_H, GLA_DK))
    o_b = o_b * lax.rsqrt(jnp.mean(jnp.square(o_b), axis=-1, keepdims=True) + LN_EPS) * gla_norm_g
    y_b = o_b.reshape(B, S, MIX_W).astype(h.dtype) * jax.nn.silu(b_r)
    y_c = stick_breaking(c_q.reshape(B, S, SB_H, SB_DH), c_k.reshape(B, S, SB_H, SB_DH),
                         c_v.reshape(B, S, SB_H, SB_DH)).reshape(B, S, MIX_W)
    xc = causal_depthwise_conv(d_x, conv_d_w, conv_d_b)
    y_d = rg_lru(xc, lru_wa, lru_ba, lru_wx, lru_bx, lru_lambda) * jax.nn.gelu(d_g)
    ys = jnp.stack([y_a, y_b, y_c, y_d], axis=2)
    proj = jnp.einsum("bsnc,ncd->bsnd", ys, w_branch)
    gates = jax.nn.sigmoid(g_merge.reshape(B, S, N_BRANCH, D))
    merged = jnp.sum(gates * proj, axis=2)
    return merged @ w_out + b_out


def memory_cross_attention(h, mem, wq, wk, wv, wo):
    B, S, D = h.shape
    M = mem.shape[1]
    q = (h @ wq).reshape(B, S, MEM_H, MEM_DH)
    k = (mem @ wk).reshape(B, M, MEM_H, MEM_DH)
    v = (mem @ wv).reshape(B, M, MEM_H, MEM_DH)
    s = jnp.einsum("bshd,bmhd->bhsm", q, k).astype(jnp.float32) * (MEM_DH ** -0.5)
    p = jax.nn.softmax(s, axis=-1).astype(v.dtype)
    o = jnp.einsum("bhsm,bmhd->bshd", p, v).reshape(B, S, D)
    return o @ wo


def moe_ffn(x2, router_w, router_b, w1, b1, w2, b2):
    T, D = x2.shape
    logits = (x2 @ router_w).astype(jnp.float32) + router_b
    top_val, top_idx = lax.top_k(logits, TOP_K)
    gate = jax.nn.softmax(top_val, axis=-1)
    n_assign = T * TOP_K
    e_flat = top_idx.reshape(n_assign)
    tok_flat = jnp.arange(n_assign, dtype=jnp.int32) // TOP_K
    order = jnp.argsort(e_flat)
    e_sorted = e_flat[order]
    tok_sorted = tok_flat[order]
    g_sorted = gate.reshape(n_assign)[order]
    counts = jnp.bincount(e_flat, length=N_EXPERTS)
    start = jnp.cumsum(counts) - counts
    padded = ((counts + MOE_BLOCK - 1) // MOE_BLOCK) * MOE_BLOCK
    pend = jnp.cumsum(padded)
    pstart = pend - padded
    dest = pstart[e_sorted] + (jnp.arange(n_assign, dtype=jnp.int32) - start[e_sorted])
    n_blocks = (n_assign + MOE_BLOCK - 1) // MOE_BLOCK + N_EXPERTS
    n_slots = n_blocks * MOE_BLOCK
    slot_tok = jnp.full((n_slots,), T, jnp.int32).at[dest].set(tok_sorted)
    slot_gate = jnp.zeros((n_slots,), x2.dtype).at[dest].set(g_sorted.astype(x2.dtype))
    block_e = jnp.minimum(jnp.searchsorted(pend, jnp.arange(n_blocks, dtype=jnp.int32) * MOE_BLOCK,
                                           side="right"), N_EXPERTS - 1)
    x_pad = jnp.concatenate([x2, jnp.zeros((1, D), x2.dtype)], axis=0)
    xs = x_pad[slot_tok].reshape(n_blocks, MOE_BLOCK, D)

    def expert_block(args):
        xb, e = args
        hcat = xb @ w1[e] + b1[e]
        g = jnp.minimum(hcat[:, :MOE_FF], SWIGLU_LIMIT)
        lin = jnp.clip(hcat[:, MOE_FF:], -SWIGLU_LIMIT, SWIGLU_LIMIT)
        act = g * jax.nn.sigmoid(SWIGLU_ALPHA * g) * (lin + 1.0)
        return act @ w2[e] + b2[e]

    ys = lax.map(expert_block, (xs, block_e)).reshape(n_slots, D)
    out = jnp.zeros((T + 1, D), x2.dtype).at[slot_tok].add(ys * slot_gate[:, None])
    return out[:T]


def setup_inputs(seed: int = 0) -> dict:
    key = jax.random.key(seed)
    ks = iter(jax.random.split(key, 64))
    f32 = jnp.float32
    L, D = DEPTH, D_MODEL
    beta = DEEPNORM_BETA

    def nrm(shape, scale):
        return scale * jax.random.normal(next(ks), shape, f32)

    def gain(shape):
        return 1.0 + nrm(shape, 0.02)

    a0 = jax.random.uniform(next(ks), (L, MIX_W), f32, minval=0.9, maxval=0.999)
    p = a0 ** (1.0 / LRU_C)
    lru_lambda = jnp.log(p) - jnp.log1p(-p)
    return {
        "x": nrm((BATCH, SEQ, D), 1.0),
        "mem": nrm((BATCH, N_MEM, D), 1.0),
        "ln0_g": gain((D,)),
        "ln0_b": nrm((D,), 0.02),
        "w_in": nrm((L, D, IN_COLS), D ** -0.5),
        "b_in": nrm((L, IN_COLS), 0.02),
        "conv_a_w": nrm((L, CONV_WIDTH, MIX_W), CONV_WIDTH ** -0.5),
        "conv_a_b": nrm((L, MIX_W), 0.02),
        "ln_a_g": gain((L, MIX_W)),
        "ln_a_b": nrm((L, MIX_W), 0.02),
        "gla_wa2": nrm((L, GLA_LOWRANK, GLA_H * GLA_DK), GLA_LOWRANK ** -0.5),
        "gla_ba": nrm((L, GLA_H * GLA_DK), 0.02),
        "gla_norm_g": gain((L, GLA_DV)),
        "conv_d_w": nrm((L, LRU_CONV, MIX_W), LRU_CONV ** -0.5),
        "conv_d_b": nrm((L, MIX_W), 0.02),
        "lru_wa": nrm((L, LRU_BLOCKS, LRU_BW, LRU_BW), LRU_BW ** -0.5),
        "lru_ba": nrm((L, MIX_W), 0.02),
        "lru_wx": nrm((L, LRU_BLOCKS, LRU_BW, LRU_BW), LRU_BW ** -0.5),
        "lru_bx": nrm((L, MIX_W), 0.02),
        "lru_lambda": lru_lambda,
        "w_branch": nrm((L, N_BRANCH, MIX_W, D), beta * MIX_W ** -0.5),
        "w_out": nrm((L, D, D), beta * D ** -0.5),
        "b_out": nrm((L, D), 0.02),
        "ln1_g": gain((L, D)),
        "ln1_b": nrm((L, D), 0.02),
        "ca_wq": nrm((L, D, D), D ** -0.5),
        "ca_wk": nrm((L, D, D), D ** -0.5),
        "ca_wv": nrm((L, D, D), beta * D ** -0.5),
        "ca_wo": nrm((L, D, D), beta * D ** -0.5),
        "ln2_g": gain((L, D)),
        "ln2_b": nrm((L, D), 0.02),
        "router_w": nrm((L, D, N_EXPERTS), D ** -0.5),
        "router_b": nrm((L, N_EXPERTS), 0.01),
        "moe_w1": nrm((L, N_EXPERTS, D, 2 * MOE_FF), beta * D ** -0.5),
        "moe_b1": nrm((L, N_EXPERTS, 2 * MOE_FF), 0.02),
        "moe_w2": nrm((L, N_EXPERTS, MOE_FF, D), beta * MOE_FF ** -0.5),
        "moe_b2": nrm((L, N_EXPERTS, D), 0.02),
        "ln3_g": gain((L, D)),
        "ln3_b": nrm((L, D), 0.02),
    }


def reference(x, mem, ln0_g, ln0_b, w_in, b_in, conv_a_w, conv_a_b, ln_a_g, ln_a_b, gla_wa2, gla_ba,
              gla_norm_g, conv_d_w, conv_d_b, lru_wa, lru_ba, lru_wx, lru_bx, lru_lambda, w_branch,
              w_out, b_out, ln1_g, ln1_b, ca_wq, ca_wk, ca_wv, ca_wo, ln2_g, ln2_b, router_w, router_b,
              moe_w1, moe_b1, moe_w2, moe_b2, ln3_g, ln3_b):
    B, S, D = x.shape
    h = layer_norm(x, ln0_g, ln0_b)
    for l in range(DEPTH):
        mix = hybrid_mixer(h, w_in[l], b_in[l], conv_a_w[l], conv_a_b[l], ln_a_g[l], ln_a_b[l],
                           gla_wa2[l], gla_ba[l], gla_norm_g[l], conv_d_w[l], conv_d_b[l],
                           lru_wa[l], lru_ba[l], lru_wx[l], lru_bx[l], lru_lambda[l],
                           w_branch[l], w_out[l], b_out[l])
        h = layer_norm(DEEPNORM_ALPHA * h + mix, ln1_g[l], ln1_b[l])
        ca = memory_cross_attention(h, mem, ca_wq[l], ca_wk[l], ca_wv[l], ca_wo[l])
        h = layer_norm(DEEPNORM_ALPHA * h + ca, ln2_g[l], ln2_b[l])
        ff = moe_ffn(h.reshape(B * S, D), router_w[l], router_b[l], moe_w1[l], moe_b1[l],
                     moe_w2[l], moe_b2[l]).reshape(B, S, D)
        h = layer_norm(DEEPNORM_ALPHA * h + ff, ln3_g[l], ln3_b[l])
    return h
```

```python
import functools

import jax
import jax.numpy as jnp
from jax import lax
from jax.experimental import pallas as pl
from jax.experimental.pallas import tpu as pltpu

F32 = jnp.float32
BF16 = jnp.bfloat16

MIX_W = 512
CONV_A_WIDTH = 31
GLA_H, GLA_DK, GLA_DV = 4, 64, 128
GLA_LOWRANK = 16
GLA_TAU = 16.0
GLA_CHUNK = 64
SB_H, SB_DH, SB_BLOCK = 8, 64, 128
LRU_C = 8.0
LRU_CONV = 4
MEM_H = 4
N_EXPERTS, TOP_K = 32, 4
MOE_BLOCK = 128
SWIGLU_LIMIT = 7.0
SWIGLU_ALPHA = 1.702
LN_EPS = 1e-5
LANES = 128
VMEM_LIMIT = 48 * 1024 * 1024

COL_A_VAL, COL_A_GATE = 0, 512
COL_B_Q, COL_B_K, COL_B_V, COL_B_R = 1024, 1280, 1536, 2048
COL_C_Q, COL_C_K, COL_C_V = 2560, 3072, 3584
COL_D_X, COL_D_G = 4096, 4608
COL_G = 5120
Z_COLS = 9216
LR_START, LR_END = 2560, 2576


def _cparams(*sem):
    return pltpu.CompilerParams(dimension_semantics=sem, vmem_limit_bytes=VMEM_LIMIT)


def _ln(x, g, b):
    mu = jnp.mean(x, axis=-1, keepdims=True)
    xc = x - mu
    var = jnp.mean(xc * xc, axis=-1, keepdims=True)
    return xc * lax.rsqrt(var + LN_EPS) * g + b


def _sigmoid(x):
    return 1.0 / (1.0 + jnp.exp(-x))


def _softplus(x):
    return jnp.maximum(x, 0.0) + jnp.log1p(jnp.exp(-jnp.abs(x)))


def _split_bf16(x):
    hi = x.astype(BF16)
    lo = (x - hi.astype(F32)).astype(BF16)
    return hi, lo


def _dot(a, b):
    return jnp.dot(a, b, preferred_element_type=F32)


def _dot_nt(a, b):
    return lax.dot_general(a, b, (((1,), (1,)), ((), ())), preferred_element_type=F32)


def _dot_tn(a, b):
    return lax.dot_general(a, b, (((0,), (0,)), ((), ())), preferred_element_type=F32)


def _ln_kernel(x_ref, g_ref, b_ref, o_ref):
    o_ref[...] = _ln(x_ref[...], g_ref[...], b_ref[...])


def _layer_norm(x, g, b, tm=512):
    t, d = x.shape
    return pl.pallas_call(
        _ln_kernel,
        grid=(t // tm,),
        in_specs=[pl.BlockSpec((tm, d), lambda i: (i, 0)),
                  pl.BlockSpec((1, d), lambda i: (0, 0)),
                  pl.BlockSpec((1, d), lambda i: (0, 0))],
        out_specs=pl.BlockSpec((tm, d), lambda i: (i, 0)),
        out_shape=jax.ShapeDtypeStruct((t, d), F32),
        compiler_params=_cparams("arbitrary"),
        name="layer_norm",
    )(x, g.reshape(1, d), b.reshape(1, d))


def _mm_kernel(a_ref, w_ref, b_ref, o_ref, abf_ref):
    @pl.when(pl.program_id(1) == 0)
    def _():
        abf_ref[...] = a_ref[...].astype(BF16)

    o_ref[...] = _dot(abf_ref[...], w_ref[...]) + b_ref[...]


def _matmul_bias(a, w_bf, b, tm, tn):
    m, k = a.shape
    n = w_bf.shape[1]
    return pl.pallas_call(
        _mm_kernel,
        grid=(m // tm, n // tn),
        in_specs=[pl.BlockSpec((tm, k), lambda i, j: (i, 0)),
                  pl.BlockSpec((k, tn), lambda i, j: (0, j)),
                  pl.BlockSpec((1, tn), lambda i, j: (0, j))],
        out_specs=pl.BlockSpec((tm, tn), lambda i, j: (i, j)),
        out_shape=jax.ShapeDtypeStruct((m, n), F32),
        scratch_shapes=[pltpu.VMEM((tm, k), BF16)],
        compiler_params=_cparams("arbitrary", "arbitrary"),
        name="matmul_bias",
    )(a, w_bf, b.reshape(1, n))


A_HALO = 32
A_ROWS = 64


def _conv_a_kernel(val_ref, gate_ref, w_ref, cb_ref, g_ref, b_ref, o_ref, ubuf, ybuf, *, tt):
    i = pl.program_id(1)

    @pl.when(i == 0)
    def _():
        ubuf[0:A_HALO, :] = jnp.zeros((A_HALO, MIX_W), F32)

    @pl.when(i > 0)
    def _():
        ubuf[0:A_HALO, :] = ubuf[tt:tt + A_HALO, :]

    ubuf[A_HALO:A_HALO + tt, :] = val_ref[...] * _sigmoid(gate_ref[...])
    off = A_HALO - (CONV_A_WIDTH - 1)
    for r0 in range(0, tt, A_ROWS):
        for c0 in range(0, MIX_W, LANES):
            acc = jnp.zeros((A_ROWS, LANES), F32) + cb_ref[:, c0:c0 + LANES]
            for j in range(CONV_A_WIDTH):
                acc = acc + w_ref[j:j + 1, c0:c0 + LANES] * ubuf[r0 + off + j:r0 + off + j + A_ROWS, c0:c0 + LANES]
            ybuf[r0:r0 + A_ROWS, c0:c0 + LANES] = acc
    y = _ln(ybuf[...], g_ref[...], b_ref[...])
    o_ref[...] = y * _sigmoid(y)


def _branch_a(z, conv_w, conv_b, ln_g, ln_b, batch, seq, tt=256):
    nt = seq // tt
    vec = lambda: pl.BlockSpec((1, MIX_W), lambda b, i: (0, 0))
    return pl.pallas_call(
        functools.partial(_conv_a_kernel, tt=tt),
        grid=(batch, nt),
        in_specs=[pl.BlockSpec((tt, MIX_W), lambda b, i: (b * nt + i, COL_A_VAL // MIX_W)),
                  pl.BlockSpec((tt, MIX_W), lambda b, i: (b * nt + i, COL_A_GATE // MIX_W)),
                  pl.BlockSpec((CONV_A_WIDTH, MIX_W), lambda b, i: (0, 0)),
                  vec(), vec(), vec()],
        out_specs=pl.BlockSpec((tt, MIX_W), lambda b, i: (b * nt + i, 0)),
        out_shape=jax.ShapeDtypeStruct((batch * seq, MIX_W), F32),
        scratch_shapes=[pltpu.VMEM((A_HALO + tt, MIX_W), F32), pltpu.VMEM((tt, MIX_W), F32)],
        compiler_params=_cparams("arbitrary", "arbitrary"),
        name="branch_a_conv",
    )(z, z, conv_w, conv_b.reshape(1, -1), ln_g.reshape(1, -1), ln_b.reshape(1, -1))


def _gla_kernel(q_ref, k_ref, v_ref, r_ref, lr_ref, wa2_ref, ba_ref, g_ref, o_ref, state, *, tt):
    @pl.when(pl.program_id(1) == 0)
    def _():
        state[...] = jnp.zeros_like(state)

    c = GLA_CHUNK
    row = lax.broadcasted_iota(jnp.int32, (c, c), 0)
    col = lax.broadcasted_iota(jnp.int32, (c, c), 1)
    causal = col <= row
    tri = jnp.where(causal, 1.0, 0.0).astype(BF16)
    ones = jnp.ones((c, LANES), BF16)
    scale = GLA_DK ** -0.5
    for c0 in range(0, tt, c):
        rows = slice(c0, c0 + c)
        x = _dot(lr_ref[rows, :].astype(BF16), wa2_ref[...]) + ba_ref[...]
        la = -_softplus(-x) / GLA_TAU
        la_hi, la_lo = _split_bf16(la)
        b = _dot(tri, la_hi) + _dot(tri, la_lo)
        b_last = b[c - 1:c, :]
        q_dec = (q_ref[rows, :] * scale) * jnp.exp(b)
        k = k_ref[rows, :]
        k_inv = (k * jnp.exp(-b)).astype(BF16)
        k_end = (k * jnp.exp(b_last - b)).astype(BF16)
        q_dec = q_dec.astype(BF16)
        dsum = _dot_tn(la_hi, ones) + _dot_tn(la_lo, ones)
        for h in range(GLA_H):
            ks = slice(h * GLA_DK, (h + 1) * GLA_DK)
            vs = slice(h * GLA_DV, (h + 1) * GLA_DV)
            v = v_ref[rows, vs].astype(BF16)
            s = jnp.where(causal, _dot_nt(q_dec[:, ks], k_inv[:, ks]), 0.0)
            s_prev = state[h]
            o = _dot(s.astype(BF16), v) + _dot(q_dec[:, ks], s_prev.astype(BF16))
            state[h] = jnp.exp(dsum[ks, :]) * s_prev + _dot_tn(k_end[:, ks], v)
            o = o * lax.rsqrt(jnp.mean(o * o, axis=-1, keepdims=True) + LN_EPS) * g_ref[...]
            rg = r_ref[rows, vs]
            o_ref[rows, vs] = o * (rg * _sigmoid(rg))


def _branch_b(z, zlr, wa2_pad_bf, gla_ba, gla_norm_g, batch, seq, tt=256):
    nt = seq // tt
    hk = GLA_H * GLA_DK
    return pl.pallas_call(
        functools.partial(_gla_kernel, tt=tt),
        grid=(batch, nt),
        in_specs=[pl.BlockSpec((tt, hk), lambda b, i: (b * nt + i, COL_B_Q // hk)),
                  pl.BlockSpec((tt, hk), lambda b, i: (b * nt + i, COL_B_K // hk)),
                  pl.BlockSpec((tt, MIX_W), lambda b, i: (b * nt + i, COL_B_V // MIX_W)),
                  pl.BlockSpec((tt, MIX_W), lambda b, i: (b * nt + i, COL_B_R // MIX_W)),
                  pl.BlockSpec((tt, LANES), lambda b, i: (b * nt + i, 0)),
                  pl.BlockSpec((LANES, hk), lambda b, i: (0, 0)),
                  pl.BlockSpec((1, hk), lambda b, i: (0, 0)),
                  pl.BlockSpec((1, GLA_DV), lambda b, i: (0, 0))],
        out_specs=pl.BlockSpec((tt, MIX_W), lambda b, i: (b * nt + i, 0)),
        out_shape=jax.ShapeDtypeStruct((batch * seq, MIX_W), F32),
        scratch_shapes=[pltpu.VMEM((GLA_H, GLA_DK, GLA_DV), F32)],
        compiler_params=_cparams("arbitrary", "arbitrary"),
        name="branch_b_gla",
    )(z, z, z, z, zlr, wa2_pad_bf, gla_ba.reshape(1, -1), gla_norm_g.reshape(1, -1))


def _sb_kernel(q_ref, k_ref, v_ref, u_ref, o_ref, kb, vb):
    qi = pl.program_id(2)
    blk = SB_BLOCK

    @pl.when(qi == 0)
    def _():
        kb[...] = k_ref[...].astype(BF16)
        vb[...] = v_ref[...].astype(BF16)

    lane = lax.broadcasted_iota(jnp.int32, (blk, LANES), 1)
    row = lax.broadcasted_iota(jnp.int32, (blk, blk), 0)
    col = lax.broadcasted_iota(jnp.int32, (blk, blk), 1)
    before = col < row
    q = q_ref[...]
    q_heads = (jnp.where(lane < SB_DH, q, 0.0).astype(BF16), jnp.where(lane >= SB_DH, q, 0.0).astype(BF16))
    u = u_ref[...]
    scale = SB_DH ** -0.5

    def tile(j, carry, masked):
        start = pl.multiple_of(j * blk, blk)
        ks = kb[pl.ds(start, blk), :]
        vs = vb[pl.ds(start, blk), :]
        out = []
        for h in range(2):
            tail_c, acc = carry[2 * h], carry[2 * h + 1]
            zz = _dot_nt(q_heads[h], ks) * scale
            sp = _softplus(zz)
            lk = -sp
            if masked:
                lk = jnp.where(before, lk, 0.0)
            lk_hi, lk_lo = _split_bf16(lk)
            tail = tail_c + _dot(lk_hi, u) + _dot(lk_lo, u)
            w = jnp.exp(zz - sp + tail)
            if masked:
                w = jnp.where(before, w, 0.0)
            acc = acc + _dot(w.astype(BF16), vs)
            tail_c = tail_c + jnp.sum(lk, axis=-1, keepdims=True)
            out += [tail_c, acc]
        return tuple(out)

    zero_c = jnp.zeros((blk, 1), F32)
    zero_a = jnp.zeros((blk, LANES), F32)
    carry = tile(qi, (zero_c, zero_a, zero_c, zero_a), True)
    carry = lax.fori_loop(0, qi, lambda n, cr: tile(qi - 1 - n, cr, False), carry)
    o_ref[...] = jnp.where(lane < SB_DH, carry[1], carry[3])


def _branch_c(z, batch, seq):
    nq = seq // SB_BLOCK
    pairs = SB_H * SB_DH // LANES
    row = lax.broadcasted_iota(jnp.int32, (SB_BLOCK, SB_BLOCK), 0)
    col = lax.broadcasted_iota(jnp.int32, (SB_BLOCK, SB_BLOCK), 1)
    u = jnp.where(row > col, 1.0, 0.0).astype(BF16)
    return pl.pallas_call(
        _sb_kernel,
        grid=(batch, pairs, nq),
        in_specs=[pl.BlockSpec((SB_BLOCK, LANES), lambda b, p, i: (b * nq + i, COL_C_Q // LANES + p)),
                  pl.BlockSpec((seq, LANES), lambda b, p, i: (b, COL_C_K // LANES + p)),
                  pl.BlockSpec((seq, LANES), lambda b, p, i: (b, COL_C_V // LANES + p)),
                  pl.BlockSpec((SB_BLOCK, SB_BLOCK), lambda b, p, i: (0, 0))],
        out_specs=pl.BlockSpec((SB_BLOCK, LANES), lambda b, p, i: (b * nq + i, p)),
        out_shape=jax.ShapeDtypeStruct((batch * seq, MIX_W), F32),
        scratch_shapes=[pltpu.VMEM((seq, LANES), BF16), pltpu.VMEM((seq, LANES), BF16)],
        compiler_params=_cparams("arbitrary", "arbitrary", "arbitrary"),
        name="branch_c_stick_breaking",
    )(z, z, z, u)


D_HALO = 8


def _lru_kernel(x_ref, gate_ref, cw_ref, cb_ref, wa_ref, ba_ref, wx_ref, bx_ref, lam_ref, o_ref,
                xbuf, abuf, ubuf, hbuf, hprev, *, tt):
    i = pl.program_id(1)

    @pl.when(i == 0)
    def _():
        xbuf[0:D_HALO, :] = jnp.zeros((D_HALO, MIX_W), F32)
        hprev[...] = jnp.zeros_like(hprev)

    @pl.when(i > 0)
    def _():
        xbuf[0:D_HALO, :] = xbuf[tt:tt + D_HALO, :]

    xbuf[D_HALO:D_HALO + tt, :] = x_ref[...]
    off = D_HALO - (LRU_CONV - 1)
    xc = jnp.zeros((tt, MIX_W), F32) + cb_ref[...]
    for j in range(LRU_CONV):
        xc = xc + cw_ref[j:j + 1, :] * xbuf[off + j:off + j + tt, :]
    xc_bf = xc.astype(BF16)
    r = _sigmoid(_dot(xc_bf, wa_ref[...]) + ba_ref[...])
    gi = _sigmoid(_dot(xc_bf, wx_ref[...]) + bx_ref[...])
    log_a = LRU_C * r * (-_softplus(-lam_ref[...]))
    a = jnp.exp(log_a)
    abuf[...] = a
    ubuf[...] = jnp.sqrt(-jnp.tanh(log_a) * (a * a + 1.0)) * (gi * xc)

    def step(t, h):
        h = abuf[pl.ds(t, 1), :] * h + ubuf[pl.ds(t, 1), :]
        hbuf[pl.ds(t, 1), :] = h
        return h

    hprev[...] = lax.fori_loop(0, tt, step, hprev[...], unroll=8)
    g = gate_ref[...]
    gelu = 0.5 * g * (1.0 + jnp.tanh(0.7978845608028654 * (g + 0.044715 * g * g * g)))
    o_ref[...] = hbuf[...] * gelu


def _branch_d(z, conv_w, conv_b, wa_bd_bf, ba, wx_bd_bf, bx, lam, batch, seq, tt=256):
    nt = seq // tt
    vec = lambda: pl.BlockSpec((1, MIX_W), lambda b, i: (0, 0))
    mat = lambda: pl.BlockSpec((MIX_W, MIX_W), lambda b, i: (0, 0))
    return pl.pallas_call(
        functools.partial(_lru_kernel, tt=tt),
        grid=(batch, nt),
        in_specs=[pl.BlockSpec((tt, MIX_W), lambda b, i: (b * nt + i, COL_D_X // MIX_W)),
                  pl.BlockSpec((tt, MIX_W), lambda b, i: (b * nt + i, COL_D_G // MIX_W)),
                  pl.BlockSpec((LRU_CONV, MIX_W), lambda b, i: (0, 0)),
                  vec(), mat(), vec(), mat(), vec(), vec()],
        out_specs=pl.BlockSpec((tt, MIX_W), lambda b, i: (b * nt + i, 0)),
        out_shape=jax.ShapeDtypeStruct((batch * seq, MIX_W), F32),
        scratch_shapes=[pltpu.VMEM((D_HALO + tt, MIX_W), F32), pltpu.VMEM((tt, MIX_W), F32),
                        pltpu.VMEM((tt, MIX_W), F32), pltpu.VMEM((tt, MIX_W), F32),
                        pltpu.VMEM((1, MIX_W), F32)],
        compiler_params=_cparams("arbitrary", "arbitrary"),
        name="branch_d_rglru",
    )(z, z, conv_w, conv_b.reshape(1, -1), wa_bd_bf, ba.reshape(1, -1), wx_bd_bf, bx.reshape(1, -1),
      lam.reshape(1, -1))


def _merge_kernel(ya_ref, yb_ref, yc_ref, yd_ref, g0_ref, g1_ref, g2_ref, g3_ref, h_ref, wb_ref, wo_ref,
                  bo_ref, lg_ref, lb_ref, o_ref, *, alpha):
    merged = None
    for n, (y_ref, g_ref) in enumerate(((ya_ref, g0_ref), (yb_ref, g1_ref), (yc_ref, g2_ref), (yd_ref, g3_ref))):
        term = _sigmoid(g_ref[...]) * _dot(y_ref[...].astype(BF16), wb_ref[n])
        merged = term if merged is None else merged + term
    mix = _dot(merged.astype(BF16), wo_ref[...]) + bo_ref[...]
    o_ref[...] = _ln(alpha * h_ref[...] + mix, lg_ref[...], lb_ref[...])


def _merge(ya, yb, yc, yd, z, h, w_branch_bf, w_out_bf, b_out, ln_g, ln_b, alpha, tm=256):
    t, d = h.shape
    ysp = lambda: pl.BlockSpec((tm, MIX_W), lambda i: (i, 0))
    gsp = lambda n: pl.BlockSpec((tm, d), lambda i: (i, COL_G // d + n))
    vec = lambda: pl.BlockSpec((1, d), lambda i: (0, 0))
    return pl.pallas_call(
        functools.partial(_merge_kernel, alpha=alpha),
        grid=(t // tm,),
        in_specs=[ysp(), ysp(), ysp(), ysp(), gsp(0), gsp(1), gsp(2), gsp(3),
                  pl.BlockSpec((tm, d), lambda i: (i, 0)),
                  pl.BlockSpec((4, MIX_W, d), lambda i: (0, 0, 0)),
                  pl.BlockSpec((d, d), lambda i: (0, 0)),
                  vec(), vec(), vec()],
        out_specs=pl.BlockSpec((tm, d), lambda i: (i, 0)),
        out_shape=jax.ShapeDtypeStruct((t, d), F32),
        compiler_params=_cparams("arbitrary"),
        name="merge_out_ln",
    )(ya, yb, yc, yd, z, z, z, z, h, w_branch_bf, w_out_bf, b_out.reshape(1, d), ln_g.reshape(1, d),
      ln_b.reshape(1, d))


def _xattn_kernel(h_ref, wq_ref, kv_ref, wo_ref, lg_ref, lb_ref, o_ref, obuf, *, alpha, d):
    h = h_ref[...]
    q = _dot(h.astype(BF16), wq_ref[...])
    dh = d // MEM_H
    scale = dh ** -0.5
    for hd in range(MEM_H):
        cs = slice(hd * dh, (hd + 1) * dh)
        k = kv_ref[:, cs].astype(BF16)
        v = kv_ref[:, d + hd * dh:d + (hd + 1) * dh].astype(BF16)
        s = _dot_nt(q[:, cs].astype(BF16), k) * scale
        s = s - jnp.max(s, axis=-1, keepdims=True)
        p = jnp.exp(s)
        p = p / jnp.sum(p, axis=-1, keepdims=True)
        obuf[:, cs] = _dot(p.astype(BF16), v)
    ca = _dot(obuf[...].astype(BF16), wo_ref[...])
    o_ref[...] = _ln(alpha * h + ca, lg_ref[...], lb_ref[...])


def _xattn(h, kv, wq_bf, wo_bf, ln_g, ln_b, alpha, seq, n_mem, tm=256):
    t, d = h.shape
    per_batch = seq // tm
    vec = lambda: pl.BlockSpec((1, d), lambda i: (0, 0))
    return pl.pallas_call(
        functools.partial(_xattn_kernel, alpha=alpha, d=d),
        grid=(t // tm,),
        in_specs=[pl.BlockSpec((tm, d), lambda i: (i, 0)),
                  pl.BlockSpec((d, d), lambda i: (0, 0)),
                  pl.BlockSpec((n_mem, 2 * d), lambda i: (i // per_batch, 0)),
                  pl.BlockSpec((d, d), lambda i: (0, 0)),
                  vec(), vec()],
        out_specs=pl.BlockSpec((tm, d), lambda i: (i, 0)),
        out_shape=jax.ShapeDtypeStruct((t, d), F32),
        scratch_shapes=[pltpu.VMEM((tm, d), F32)],
        compiler_params=_cparams("arbitrary"),
        name="xattn_ln",
    )(h, wq_bf, kv, wo_bf, ln_g.reshape(1, d), ln_b.reshape(1, d))


ROUTE_IDX, ROUTE_GATE, ROUTE_RANK = 0, 4, 8


def _router_kernel(h_ref, whi_ref, wlo_ref, b_ref, lt_ref, route_ref, cnt_ref, cnt):
    @pl.when(pl.program_id(0) == 0)
    def _():
        cnt[...] = jnp.zeros_like(cnt)

    x_hi, x_lo = _split_bf16(h_ref[...])
    logits = _dot(x_hi, whi_ref[...]) + _dot(x_lo, whi_ref[...]) + _dot(x_hi, wlo_ref[...]) + b_ref[...]
    tm = logits.shape[0]
    lane = lax.broadcasted_iota(jnp.int32, (tm, LANES), 1).astype(F32)
    cur = logits
    vals, idxs = [], []
    for _ in range(TOP_K):
        m = jnp.max(cur, axis=-1, keepdims=True)
        idx = jnp.min(jnp.where(cur == m, lane, float(LANES)), axis=-1, keepdims=True)
        vals.append(m)
        idxs.append(idx)
        cur = jnp.where(lane == idx, -jnp.inf, cur)
    ex = [jnp.exp(v - vals[0]) for v in vals]
    denom = ex[0] + ex[1] + ex[2] + ex[3]
    onehot = jnp.zeros((tm, LANES), F32)
    for idx in idxs:
        onehot = onehot + jnp.where(lane == idx, 1.0, 0.0)
    before = _dot(lt_ref[...], onehot.astype(BF16)) + cnt[...]
    route = jnp.zeros((tm, LANES), F32)
    for k in range(TOP_K):
        rank = jnp.sum(jnp.where(lane == idxs[k], before, 0.0), axis=-1, keepdims=True)
        route = jnp.where(lane == float(ROUTE_IDX + k), idxs[k], route)
        route = jnp.where(lane == float(ROUTE_GATE + k), ex[k] / denom, route)
        route = jnp.where(lane == float(ROUTE_RANK + k), rank, route)
    route_ref[...] = route
    cnt[...] = cnt[...] + jnp.sum(onehot, axis=0, keepdims=True)
    cnt_ref[...] = cnt[...]


def _router(h, router_w, router_b, tm=256):
    t, d = h.shape
    w_pad = jnp.zeros((d, LANES), F32).at[:, :N_EXPERTS].set(router_w)
    w_hi = w_pad.astype(BF16)
    w_lo = (w_pad - w_hi.astype(F32)).astype(BF16)
    b_pad = jnp.full((1, LANES), -1e30, F32).at[0, :N_EXPERTS].set(router_b)
    row = lax.broadcasted_iota(jnp.int32, (tm, tm), 0)
    col = lax.broadcasted_iota(jnp.int32, (tm, tm), 1)
    lower = jnp.where(col < row, 1.0, 0.0).astype(BF16)
    return pl.pallas_call(
        _router_kernel,
        grid=(t // tm,),
        in_specs=[pl.BlockSpec((tm, d), lambda i: (i, 0)),
                  pl.BlockSpec((d, LANES), lambda i: (0, 0)),
                  pl.BlockSpec((d, LANES), lambda i: (0, 0)),
                  pl.BlockSpec((1, LANES), lambda i: (0, 0)),
                  pl.BlockSpec((tm, tm), lambda i: (0, 0))],
        out_specs=[pl.BlockSpec((tm, LANES), lambda i: (i, 0)),
                   pl.BlockSpec((1, LANES), lambda i: (0, 0))],
        out_shape=[jax.ShapeDtypeStruct((t, LANES), F32), jax.ShapeDtypeStruct((1, LANES), F32)],
        scratch_shapes=[pltpu.VMEM((1, LANES), F32)],
        compiler_params=_cparams("arbitrary"),
        name="moe_router",
    )(h, w_hi, w_lo, b_pad, lower)


DISPATCH_TOKENS = 64


def _dispatch_kernel(dest_ref, pad_lo_ref, pad_hi_ref, h_hbm, xs_hbm, zero_row, sems, *, n_tok):
    zero_row[...] = jnp.zeros_like(zero_row)
    n_groups = n_tok // DISPATCH_TOKENS

    def row_copy(t, k, sem):
        return pltpu.make_async_copy(h_hbm.at[pl.ds(t, 1)], xs_hbm.at[pl.ds(dest_ref[t * TOP_K + k], 1)], sem)

    def issue(g, sem):
        def body(n, _):
            t = g * DISPATCH_TOKENS + n
            for k in range(TOP_K):
                row_copy(t, k, sem).start()
            return 0
        lax.fori_loop(0, DISPATCH_TOKENS, body, 0)

    def drain(g, sem):
        def body(n, _):
            t = g * DISPATCH_TOKENS + n
            for k in range(TOP_K):
                row_copy(t, k, sem).wait()
            return 0
        lax.fori_loop(0, DISPATCH_TOKENS, body, 0)

    issue(0, sems.at[0])

    def group(g, _):
        slot = lax.rem(g, 2)

        @pl.when(g + 1 < n_groups)
        def _():
            issue(g + 1, sems.at[1 - slot])

        drain(g, sems.at[slot])
        return 0

    lax.fori_loop(0, n_groups, group, 0)

    def pad_copy(s):
        return pltpu.make_async_copy(zero_row, xs_hbm.at[pl.ds(s, 1)], sems.at[2])

    def expert(e, _):
        lo, hi = pad_lo_ref[e], pad_hi_ref[e]
        lax.fori_loop(lo, hi, lambda s, c: (pad_copy(s).start(), c)[1], 0)
        lax.fori_loop(lo, hi, lambda s, c: (pad_copy(s).wait(), c)[1], 0)
        return 0

    lax.fori_loop(0, N_EXPERTS, expert, 0)


def _dispatch(h, dest_flat, pad_lo, pad_hi, n_slots):
    t, d = h.shape
    return pl.pallas_call(
        functools.partial(_dispatch_kernel, n_tok=t),
        grid_spec=pltpu.PrefetchScalarGridSpec(
            num_scalar_prefetch=3,
            grid=(1,),
            in_specs=[pl.BlockSpec(memory_space=pl.ANY)],
            out_specs=pl.BlockSpec(memory_space=pl.ANY),
            scratch_shapes=[pltpu.VMEM((1, d), F32), pltpu.SemaphoreType.DMA((3,))],
        ),
        out_shape=jax.ShapeDtypeStruct((n_slots, d), F32),
        compiler_params=pltpu.CompilerParams(dimension_semantics=("arbitrary",), has_side_effects=True),
        name="moe_dispatch",
    )(dest_flat, pad_lo, pad_hi, h)


def _expert_kernel(be_ref, nu_ref, x_ref, w1_ref, b1_ref, w2_ref, b2_ref, o_ref, *, ff):
    @pl.when(pl.program_id(0) < nu_ref[0])
    def _():
        hcat = _dot(x_ref[...].astype(BF16), w1_ref[0]) + b1_ref[0]
        g = jnp.minimum(hcat[:, :ff], SWIGLU_LIMIT)
        lin = jnp.clip(hcat[:, ff:], -SWIGLU_LIMIT, SWIGLU_LIMIT)
        act = g * _sigmoid(SWIGLU_ALPHA * g) * (lin + 1.0)
        o_ref[...] = _dot(act.astype(BF16), w2_ref[0]) + b2_ref[0]


def _experts(xs, block_e, n_used, w1_bf, b1, w2_bf, b2):
    n_slots, d = xs.shape
    ff = w2_bf.shape[1]
    n_blocks = n_slots // MOE_BLOCK
    last = lambda i, nu: jnp.minimum(i, nu[0] - 1)
    return pl.pallas_call(
        functools.partial(_expert_kernel, ff=ff),
        grid_spec=pltpu.PrefetchScalarGridSpec(
            num_scalar_prefetch=2,
            grid=(n_blocks,),
            in_specs=[pl.BlockSpec((MOE_BLOCK, d), lambda i, be, nu: (last(i, nu), 0)),
                      pl.BlockSpec((1, d, 2 * ff), lambda i, be, nu: (be[i], 0, 0)),
                      pl.BlockSpec((1, 1, 2 * ff), lambda i, be, nu: (be[i], 0, 0)),
                      pl.BlockSpec((1, ff, d), lambda i, be, nu: (be[i], 0, 0)),
                      pl.BlockSpec((1, 1, d), lambda i, be, nu: (be[i], 0, 0))],
            out_specs=pl.BlockSpec((MOE_BLOCK, d), lambda i, be, nu: (last(i, nu), 0)),
        ),
        out_shape=jax.ShapeDtypeStruct((n_slots, d), F32),
        compiler_params=_cparams("arbitrary"),
        name="moe_experts",
    )(block_e, n_used, xs, w1_bf, b1.reshape(N_EXPERTS, 1, 2 * ff), w2_bf, b2.reshape(N_EXPERTS, 1, d))


COMBINE_TOKENS = 128


def _combine_kernel(dest_ref, ys_hbm, route_ref, h_ref, lg_ref, lb_ref, o_ref, buf, sems, *, alpha):
    i = pl.program_id(0)
    n = pl.num_programs(0)
    tm = COMBINE_TOKENS

    def row_copy(step, slot, t, k):
        src = dest_ref[(step * tm + t) * TOP_K + k]
        return pltpu.make_async_copy(ys_hbm.at[pl.ds(src, 1)], buf.at[slot, k, pl.ds(t, 1)], sems.at[slot])

    def issue(step, slot):
        def body(t, _):
            for k in range(TOP_K):
                row_copy(step, slot, t, k).start()
            return 0
        lax.fori_loop(0, tm, body, 0)

    def drain(step, slot):
        def body(t, _):
            for k in range(TOP_K):
                row_copy(step, slot, t, k).wait()
            return 0
        lax.fori_loop(0, tm, body, 0)

    slot = lax.rem(i, 2)

    @pl.when(i == 0)
    def _():
        issue(0, 0)

    @pl.when(i + 1 < n)
    def _():
        issue(i + 1, 1 - slot)

    drain(i, slot)
    route = route_ref[...]
    ff = jnp.zeros(o_ref.shape, F32)
    for k in range(TOP_K):
        ff = ff + route[:, ROUTE_GATE + k:ROUTE_GATE + k + 1] * buf[slot, k]
    o_ref[...] = _ln(alpha * h_ref[...] + ff, lg_ref[...], lb_ref[...])


def _combine(ys, dest_flat, route, h, ln_g, ln_b, alpha):
    t, d = h.shape
    tm = COMBINE_TOKENS
    vec = lambda: pl.BlockSpec((1, d), lambda i, dst: (0, 0))
    return pl.pallas_call(
        functools.partial(_combine_kernel, alpha=alpha),
        grid_spec=pltpu.PrefetchScalarGridSpec(
            num_scalar_prefetch=1,
            grid=(t // tm,),
            in_specs=[pl.BlockSpec(memory_space=pl.ANY),
                      pl.BlockSpec((tm, LANES), lambda i, dst: (i, 0)),
                      pl.BlockSpec((tm, d), lambda i, dst: (i, 0)),
                      vec(), vec()],
            out_specs=pl.BlockSpec((tm, d), lambda i, dst: (i, 0)),
            scratch_shapes=[pltpu.VMEM((2, TOP_K, tm, d), F32), pltpu.SemaphoreType.DMA((2,))],
        ),
        out_shape=jax.ShapeDtypeStruct((t, d), F32),
        compiler_params=_cparams("arbitrary"),
        name="moe_combine_ln",
    )(dest_flat, ys, route, h, ln_g.reshape(1, d), ln_b.reshape(1, d))


def _moe(h, router_w, router_b, w1_bf, b1, w2_bf, b2, ln_g, ln_b, alpha):
    t, d = h.shape
    route, cnt = _router(h, router_w, router_b)
    idx = route[:, ROUTE_IDX:ROUTE_IDX + TOP_K].astype(jnp.int32)
    rank = route[:, ROUTE_RANK:ROUTE_RANK + TOP_K].astype(jnp.int32)
    counts = cnt[0, :N_EXPERTS].astype(jnp.int32)
    padded = ((counts + MOE_BLOCK - 1) // MOE_BLOCK) * MOE_BLOCK
    pend = jnp.cumsum(padded)
    pstart = pend - padded
    dest = (pstart[idx] + rank).reshape(t * TOP_K)
    n_blocks = t * TOP_K // MOE_BLOCK + N_EXPERTS
    n_used = (pend[-1:] // MOE_BLOCK).astype(jnp.int32)
    block_e = jnp.minimum(jnp.searchsorted(pend, jnp.arange(n_blocks, dtype=jnp.int32) * MOE_BLOCK, side="right"),
                          N_EXPERTS - 1).astype(jnp.int32)
    xs = _dispatch(h, dest, (pstart + counts).astype(jnp.int32), pend.astype(jnp.int32), n_blocks * MOE_BLOCK)
    ys = _experts(xs, block_e, n_used, w1_bf, b1, w2_bf, b2)
    return _combine(ys, dest, route, h, ln_g, ln_b, alpha)


def _block_diag(w):
    n, bw, _ = w.shape
    eye = jnp.eye(n, dtype=w.dtype)
    return (eye[:, None, :, None] * w[:, :, None, :]).reshape(n * bw, n * bw)


def kernel(x, mem, ln0_g, ln0_b, w_in, b_in, conv_a_w, conv_a_b, ln_a_g, ln_a_b, gla_wa2, gla_ba, gla_norm_g, conv_d_w, conv_d_b, lru_wa, lru_ba, lru_wx, lru_bx, lru_lambda, w_branch, w_out, b_out, ln1_g, ln1_b, ca_wq, ca_wk, ca_wv, ca_wo, ln2_g, ln2_b, router_w, router_b, moe_w1, moe_b1, moe_w2, moe_b2, ln3_g, ln3_b):
    batch, seq, d = x.shape
    n_mem = mem.shape[1]
    depth = w_in.shape[0]
    alpha = (2 * depth) ** 0.25
    t = batch * seq
    h = _layer_norm(x.reshape(t, d), ln0_g, ln0_b)
    mem2 = mem.reshape(batch * n_mem, d)
    for l in range(depth):
        w_main = jnp.concatenate([w_in[l][:, :LR_START], w_in[l][:, LR_END:]], axis=1).astype(BF16)
        b_main = jnp.concatenate([b_in[l][:LR_START], b_in[l][LR_END:]])
        w_lr = jnp.zeros((d, LANES), F32).at[:, :GLA_LOWRANK].set(w_in[l][:, LR_START:LR_END]).astype(BF16)
        b_lr = jnp.zeros((LANES,), F32).at[:GLA_LOWRANK].set(b_in[l][LR_START:LR_END])
        z = _matmul_bias(h, w_main, b_main, tm=1024, tn=512)
        zlr = _matmul_bias(h, w_lr, b_lr, tm=1024, tn=LANES)
        wa2_pad = jnp.zeros((LANES, GLA_H * GLA_DK), F32).at[:GLA_LOWRANK].set(gla_wa2[l]).astype(BF16)
        ya = _branch_a(z, conv_a_w[l], conv_a_b[l], ln_a_g[l], ln_a_b[l], batch, seq)
        yb = _branch_b(z, zlr, wa2_pad, gla_ba[l], gla_norm_g[l], batch, seq)
        yc = _branch_c(z, batch, seq)
        yd = _branch_d(z, conv_d_w[l], conv_d_b[l], _block_diag(lru_wa[l]).astype(BF16), lru_ba[l],
                       _block_diag(lru_wx[l]).astype(BF16), lru_bx[l], lru_lambda[l], batch, seq)
        h = _merge(ya, yb, yc, yd, z, h, w_branch[l].astype(BF16), w_out[l].astype(BF16), b_out[l],
                   ln1_g[l], ln1_b[l], alpha)
        w_kv = jnp.concatenate([ca_wk[l], ca_wv[l]], axis=1).astype(BF16)
        kv = _matmul_bias(mem2, w_kv, jnp.zeros((2 * d,), F32), tm=batch * n_mem, tn=512)
        h = _xattn(h, kv, ca_wq[l].astype(BF16), ca_wo[l].astype(BF16), ln2_g[l], ln2_b[l], alpha, seq, n_mem)
        h = _moe(h, router_w[l], router_b[l], moe_w1[l].astype(BF16), moe_b1[l], moe_w2[l].astype(BF16),
                 moe_b2[l], ln3_g[l], ln3_b[l], alpha)
    return h.reshape(batch, seq, d)
```

```python
import functools

import jax
import jax.numpy as jnp
from jax import lax
from jax.experimental import pallas as pl
from jax.experimental.pallas import tpu as pltpu

F32 = jnp.float32
BF16 = jnp.bfloat16

MIX_W = 512
CONV_A_WIDTH = 31
GLA_H, GLA_DK, GLA_DV = 4, 64, 128
GLA_LOWRANK = 16
GLA_TAU = 16.0
GLA_CHUNK = 64
SB_H, SB_DH, SB_BLOCK = 8, 64, 128
LRU_C = 8.0
LRU_CONV = 4
MEM_H = 4
N_EXPERTS, TOP_K = 32, 4
MOE_BLOCK = 128
SWIGLU_LIMIT = 7.0
SWIGLU_ALPHA = 1.702
LN_EPS = 1e-5
LANES = 128
VMEM_LIMIT = 48 * 1024 * 1024

COL_A_VAL, COL_A_GATE = 0, 512
COL_B_Q, COL_B_K, COL_B_V, COL_B_R = 1024, 1280, 1536, 2048
COL_C_Q, COL_C_K, COL_C_V = 2560, 3072, 3584
COL_D_X, COL_D_G = 4096, 4608
COL_G = 5120
Z_COLS = 9216
LR_START, LR_END = 2560, 2576


def _cparams(*sem):
    return pltpu.CompilerParams(dimension_semantics=sem, vmem_limit_bytes=VMEM_LIMIT)


def _ln(x, g, b):
    mu = jnp.mean(x, axis=-1, keepdims=True)
    xc = x - mu
    var = jnp.mean(xc * xc, axis=-1, keepdims=True)
    return xc * lax.rsqrt(var + LN_EPS) * g + b


def _sigmoid(x):
    return 1.0 / (1.0 + jnp.exp(-x))


def _softplus(x):
    return jnp.maximum(x, 0.0) + jnp.log1p(jnp.exp(-jnp.abs(x)))


def _split_bf16(x):
    hi = x.astype(BF16)
    lo = (x - hi.astype(F32)).astype(BF16)
    return hi, lo


def _dot(a, b):
    return jnp.dot(a, b, preferred_element_type=F32)


def _dot_nt(a, b):
    return lax.dot_general(a, b, (((1,), (1,)), ((), ())), preferred_element_type=F32)


def _dot_tn(a, b):
    return lax.dot_general(a, b, (((0,), (0,)), ((), ())), preferred_element_type=F32)


def _ln_kernel(x_ref, g_ref, b_ref, o_ref):
    o_ref[...] = _ln(x_ref[...], g_ref[...], b_ref[...])


def _layer_norm(x, g, b, tm=512):
    t, d = x.shape
    return pl.pallas_call(
        _ln_kernel,
        grid=(t // tm,),
        in_specs=[pl.BlockSpec((tm, d), lambda i: (i, 0)),
                  pl.BlockSpec((1, d), lambda i: (0, 0)),
                  pl.BlockSpec((1, d), lambda i: (0, 0))],
        out_specs=pl.BlockSpec((tm, d), lambda i: (i, 0)),
        out_shape=jax.ShapeDtypeStruct((t, d), F32),
        compiler_params=_cparams("arbitrary"),
        name="layer_norm",
    )(x, g.reshape(1, d), b.reshape(1, d))


def _mm_kernel(a_ref, w_ref, b_ref, o_ref, abf_ref):
    @pl.when(pl.program_id(1) == 0)
    def _():
        abf_ref[...] = a_ref[...].astype(BF16)

    o_ref[...] = _dot(abf_ref[...], w_ref[...]) + b_ref[...]


def _matmul_bias(a, w_bf, b, tm, tn):
    m, k = a.shape
    n = w_bf.shape[1]
    return pl.pallas_call(
        _mm_kernel,
        grid=(m // tm, n // tn),
        in_specs=[pl.BlockSpec((tm, k), lambda i, j: (i, 0)),
                  pl.BlockSpec((k, tn), lambda i, j: (0, j)),
                  pl.BlockSpec((1, tn), lambda i, j: (0, j))],
        out_specs=pl.BlockSpec((tm, tn), lambda i, j: (i, j)),
        out_shape=jax.ShapeDtypeStruct((m, n), F32),
        scratch_shapes=[pltpu.VMEM((tm, k), BF16)],
        compiler_params=_cparams("arbitrary", "arbitrary"),
        name="matmul_bias",
    )(a, w_bf, b.reshape(1, n))


A_HALO = 32
A_ROWS = 64


def _conv_a_kernel(val_ref, gate_ref, w_ref, cb_ref, g_ref, b_ref, o_ref, ubuf, ybuf, *, tt):
    i = pl.program_id(1)

    @pl.when(i == 0)
    def _():
        ubuf[0:A_HALO, :] = jnp.zeros((A_HALO, MIX_W), F32)

    @pl.when(i > 0)
    def _():
        ubuf[0:A_HALO, :] = ubuf[tt:tt + A_HALO, :]

    ubuf[A_HALO:A_HALO + tt, :] = val_ref[...] * _sigmoid(gate_ref[...])
    off = A_HALO - (CONV_A_WIDTH - 1)
    for r0 in range(0, tt, A_ROWS):
        for c0 in range(0, MIX_W, LANES):
            acc = jnp.zeros((A_ROWS, LANES), F32) + cb_ref[:, c0:c0 + LANES]
            for j in range(CONV_A_WIDTH):
                acc = acc + w_ref[j:j + 1, c0:c0 + LANES] * ubuf[r0 + off + j:r0 + off + j + A_ROWS, c0:c0 + LANES]
            ybuf[r0:r0 + A_ROWS, c0:c0 + LANES] = acc
    y = _ln(ybuf[...], g_ref[...], b_ref[...])
    o_ref[...] = y * _sigmoid(y)


def _branch_a(z, conv_w, conv_b, ln_g, ln_b, batch, seq, tt=256):
    nt = seq // tt
    vec = lambda: pl.BlockSpec((1, MIX_W), lambda b, i: (0, 0))
    return pl.pallas_call(
        functools.partial(_conv_a_kernel, tt=tt),
        grid=(batch, nt),
        in_specs=[pl.BlockSpec((tt, MIX_W), lambda b, i: (b * nt + i, COL_A_VAL // MIX_W)),
                  pl.BlockSpec((tt, MIX_W), lambda b, i: (b * nt + i, COL_A_GATE // MIX_W)),
                  pl.BlockSpec((CONV_A_WIDTH, MIX_W), lambda b, i: (0, 0)),
                  vec(), vec(), vec()],
        out_specs=pl.BlockSpec((tt, MIX_W), lambda b, i: (b * nt + i, 0)),
        out_shape=jax.ShapeDtypeStruct((batch * seq, MIX_W), F32),
        scratch_shapes=[pltpu.VMEM((A_HALO + tt, MIX_W), F32), pltpu.VMEM((tt, MIX_W), F32)],
        compiler_params=_cparams("arbitrary", "arbitrary"),
        name="branch_a_conv",
    )(z, z, conv_w, conv_b.reshape(1, -1), ln_g.reshape(1, -1), ln_b.reshape(1, -1))


def _gla_kernel(q_ref, k_ref, v_ref, r_ref, lr_ref, wa2_ref, ba_ref, g_ref, o_ref, state, *, tt):
    @pl.when(pl.program_id(1) == 0)
    def _():
        state[...] = jnp.zeros_like(state)

    c = GLA_CHUNK
    row = lax.broadcasted_iota(jnp.int32, (c, c), 0)
    col = lax.broadcasted_iota(jnp.int32, (c, c), 1)
    causal = col <= row
    tri = jnp.where(causal, 1.0, 0.0).astype(BF16)
    ones = jnp.ones((c, LANES), BF16)
    scale = GLA_DK ** -0.5
    for c0 in range(0, tt, c):
        rows = slice(c0, c0 + c)
        x = _dot(lr_ref[rows, :].astype(BF16), wa2_ref[...]) + ba_ref[...]
        la = -_softplus(-x) / GLA_TAU
        la_hi, la_lo = _split_bf16(la)
        b = _dot(tri, la_hi) + _dot(tri, la_lo)
        b_last = b[c - 1:c, :]
        q_dec = (q_ref[rows, :] * scale) * jnp.exp(b)
        k = k_ref[rows, :]
        k_inv = (k * jnp.exp(-b)).astype(BF16)
        k_end = (k * jnp.exp(b_last - b)).astype(BF16)
        q_dec = q_dec.astype(BF16)
        dsum = _dot_tn(la_hi, ones) + _dot_tn(la_lo, ones)
        for h in range(GLA_H):
            ks = slice(h * GLA_DK, (h + 1) * GLA_DK)
            vs = slice(h * GLA_DV, (h + 1) * GLA_DV)
            v = v_ref[rows, vs].astype(BF16)
            s = jnp.where(causal, _dot_nt(q_dec[:, ks], k_inv[:, ks]), 0.0)
            s_prev = state[h]
            o = _dot(s.astype(BF16), v) + _dot(q_dec[:, ks], s_prev.astype(BF16))
            state[h] = jnp.exp(dsum[ks, :]) * s_prev + _dot_tn(k_end[:, ks], v)
            o = o * lax.rsqrt(jnp.mean(o * o, axis=-1, keepdims=True) + LN_EPS) * g_ref[...]
            rg = r_ref[rows, vs]
            o_ref[rows, vs] = o * (rg * _sigmoid(rg))


def _branch_b(z, zlr, wa2_pad_bf, gla_ba, gla_norm_g, batch, seq, tt=256):
    nt = seq // tt
    hk = GLA_H * GLA_DK
    return pl.pallas_call(
        functools.partial(_gla_kernel, tt=tt),
        grid=(batch, nt),
        in_specs=[pl.BlockSpec((tt, hk), lambda b, i: (b * nt + i, COL_B_Q // hk)),
                  pl.BlockSpec((tt, hk), lambda b, i: (b * nt + i, COL_B_K // hk)),
                  pl.BlockSpec((tt, MIX_W), lambda b, i: (b * nt + i, COL_B_V // MIX_W)),
                  pl.BlockSpec((tt, MIX_W), lambda b, i: (b * nt + i, COL_B_R // MIX_W)),
                  pl.BlockSpec((tt, LANES), lambda b, i: (b * nt + i, 0)),
                  pl.BlockSpec((LANES, hk), lambda b, i: (0, 0)),
                  pl.BlockSpec((1, hk), lambda b, i: (0, 0)),
                  pl.BlockSpec((1, GLA_DV), lambda b, i: (0, 0))],
        out_specs=pl.BlockSpec((tt, MIX_W), lambda b, i: (b * nt + i, 0)),
        out_shape=jax.ShapeDtypeStruct((batch * seq, MIX_W), F32),
        scratch_shapes=[pltpu.VMEM((GLA_H, GLA_DK, GLA_DV), F32)],
        compiler_params=_cparams("arbitrary", "arbitrary"),
        name="branch_b_gla",
    )(z, z, z, z, zlr, wa2_pad_bf, gla_ba.reshape(1, -1), gla_norm_g.reshape(1, -1))


SB_PAIRS = 4


def _sb_kernel(q_ref, k_ref, v_ref, uj_ref, o_ref, kb, vb, ls_s, lk_s, w_s, carry_s, acc_s):
    qi = pl.program_id(2)
    blk = SB_BLOCK
    rows = pl.ds(pl.multiple_of(qi * blk, blk), blk)
    kb[rows, :] = k_ref[...].astype(BF16)
    vb[rows, :] = v_ref[...].astype(BF16)

    lane = lax.broadcasted_iota(jnp.int32, (blk, LANES), 1)
    row2 = lax.broadcasted_iota(jnp.int32, (2 * blk, blk), 0)
    col2 = lax.broadcasted_iota(jnp.int32, (2 * blk, blk), 1)
    before = col2 < jnp.bitwise_and(row2, blk - 1)
    scale = SB_DH ** -0.5
    qs = []
    for p in range(SB_PAIRS):
        q = q_ref[:, p * LANES:(p + 1) * LANES] * scale
        qs.append(jnp.concatenate([jnp.where(lane < SB_DH, q, 0.0).astype(BF16),
                                   jnp.where(lane >= SB_DH, q, 0.0).astype(BF16)], axis=0))
        carry_s[p] = jnp.zeros((2 * blk, LANES), F32)
        acc_s[p] = jnp.zeros((2 * blk, LANES), F32)

    def tile_rows(j):
        return pl.ds(pl.multiple_of(j * blk, blk), blk)

    def stage_a(j, masked):
        for p in range(SB_PAIRS):
            ks = kb[tile_rows(j), p * LANES:(p + 1) * LANES]
            zz = _dot_nt(qs[p], ks)
            neg = -zz
            lk = jnp.minimum(neg, 0.0) - jnp.log(1.0 + jnp.exp(jnp.minimum(zz, neg)))
            ls = zz + lk
            if masked:
                lk = jnp.where(before, lk, 0.0)
                ls = jnp.where(before, ls, -1e30)
            hi, lo = _split_bf16(lk)
            ls_s[p] = ls
            lk_s[p] = jnp.concatenate([hi, lo], axis=1)

    def stage_b():
        for p in range(SB_PAIRS):
            r = _dot(lk_s[p], uj_ref[...])
            carry = carry_s[p]
            w_s[p] = jnp.exp(ls_s[p] + carry + r[:, :LANES]).astype(BF16)
            carry_s[p] = carry + r[:, LANES:]

    def stage_c(j):
        for p in range(SB_PAIRS):
            acc_s[p] = acc_s[p] + _dot(w_s[p], vb[tile_rows(j), p * LANES:(p + 1) * LANES])

    stage_a(qi, True)

    @pl.when(qi >= 1)
    def _():
        stage_b()
        stage_a(qi - 1, False)

    def body(m, _):
        stage_c(qi - m + 2)
        stage_b()
        stage_a(qi - m, False)
        return 0

    lax.fori_loop(2, qi + 1, body, 0)

    @pl.when(qi >= 1)
    def _():
        stage_c(1)

    stage_b()
    stage_c(0)
    for p in range(SB_PAIRS):
        acc = acc_s[p]
        o_ref[:, p * LANES:(p + 1) * LANES] = jnp.where(lane < SB_DH, acc[:blk], acc[blk:])


def _branch_c(z, batch, seq):
    nq = seq // SB_BLOCK
    width = SB_PAIRS * LANES
    groups = SB_H * SB_DH // width
    blk = SB_BLOCK
    row = lax.broadcasted_iota(jnp.int32, (blk, blk), 0)
    col = lax.broadcasted_iota(jnp.int32, (blk, blk), 1)
    u = jnp.where(row > col, 1.0, 0.0)
    uj = jnp.concatenate([u, jnp.ones((blk, blk), F32)], axis=1)
    uj = jnp.concatenate([uj, uj], axis=0).astype(BF16)
    blockspec = lambda col0: pl.BlockSpec((blk, width), lambda b, g, i: (b * nq + i, col0 // width + g))
    stage = lambda lanes, dt: pltpu.VMEM((SB_PAIRS, 2 * blk, lanes), dt)
    return pl.pallas_call(
        _sb_kernel,
        grid=(batch, groups, nq),
        in_specs=[blockspec(COL_C_Q), blockspec(COL_C_K), blockspec(COL_C_V),
                  pl.BlockSpec((2 * blk, 2 * blk), lambda b, g, i: (0, 0))],
        out_specs=pl.BlockSpec((blk, width), lambda b, g, i: (b * nq + i, g)),
        out_shape=jax.ShapeDtypeStruct((batch * seq, MIX_W), F32),
        scratch_shapes=[pltpu.VMEM((seq, width), BF16), pltpu.VMEM((seq, width), BF16),
                        stage(LANES, F32), stage(2 * LANES, BF16), stage(LANES, BF16),
                        stage(LANES, F32), stage(LANES, F32)],
        compiler_params=_cparams("arbitrary", "arbitrary", "arbitrary"),
        name="branch_c_stick_breaking",
    )(z, z, z, uj)


D_HALO = 8


def _lru_kernel(x_ref, gate_ref, cw_ref, cb_ref, wa_ref, ba_ref, wx_ref, bx_ref, lam_ref, o_ref,
                xbuf, abuf, ubuf, hbuf, hprev, *, tt):
    i = pl.program_id(1)

    @pl.when(i == 0)
    def _():
        xbuf[0:D_HALO, :] = jnp.zeros((D_HALO, MIX_W), F32)
        hprev[...] = jnp.zeros_like(hprev)

    @pl.when(i > 0)
    def _():
        xbuf[0:D_HALO, :] = xbuf[tt:tt + D_HALO, :]

    xbuf[D_HALO:D_HALO + tt, :] = x_ref[...]
    off = D_HALO - (LRU_CONV - 1)
    xc = jnp.zeros((tt, MIX_W), F32) + cb_ref[...]
    for j in range(LRU_CONV):
        xc = xc + cw_ref[j:j + 1, :] * xbuf[off + j:off + j + tt, :]
    xc_bf = xc.astype(BF16)
    r = _sigmoid(_dot(xc_bf, wa_ref[...]) + ba_ref[...])
    gi = _sigmoid(_dot(xc_bf, wx_ref[...]) + bx_ref[...])
    log_a = LRU_C * r * (-_softplus(-lam_ref[...]))
    a = jnp.exp(log_a)
    abuf[...] = a
    ubuf[...] = jnp.sqrt(-jnp.tanh(log_a) * (a * a + 1.0)) * (gi * xc)

    def step(t, h):
        h = abuf[pl.ds(t, 1), :] * h + ubuf[pl.ds(t, 1), :]
        hbuf[pl.ds(t, 1), :] = h
        return h

    hprev[...] = lax.fori_loop(0, tt, step, hprev[...], unroll=8)
    g = gate_ref[...]
    gelu = 0.5 * g * (1.0 + jnp.tanh(0.7978845608028654 * (g + 0.044715 * g * g * g)))
    o_ref[...] = hbuf[...] * gelu


def _branch_d(z, conv_w, conv_b, wa_bd_bf, ba, wx_bd_bf, bx, lam, batch, seq, tt=256):
    nt = seq // tt
    vec = lambda: pl.BlockSpec((1, MIX_W), lambda b, i: (0, 0))
    mat = lambda: pl.BlockSpec((MIX_W, MIX_W), lambda b, i: (0, 0))
    return pl.pallas_call(
        functools.partial(_lru_kernel, tt=tt),
        grid=(batch, nt),
        in_specs=[pl.BlockSpec((tt, MIX_W), lambda b, i: (b * nt + i, COL_D_X // MIX_W)),
                  pl.BlockSpec((tt, MIX_W), lambda b, i: (b * nt + i, COL_D_G // MIX_W)),
                  pl.BlockSpec((LRU_CONV, MIX_W), lambda b, i: (0, 0)),
                  vec(), mat(), vec(), mat(), vec(), vec()],
        out_specs=pl.BlockSpec((tt, MIX_W), lambda b, i: (b * nt + i, 0)),
        out_shape=jax.ShapeDtypeStruct((batch * seq, MIX_W), F32),
        scratch_shapes=[pltpu.VMEM((D_HALO + tt, MIX_W), F32), pltpu.VMEM((tt, MIX_W), F32),
                        pltpu.VMEM((tt, MIX_W), F32), pltpu.VMEM((tt, MIX_W), F32),
                        pltpu.VMEM((1, MIX_W), F32)],
        compiler_params=_cparams("arbitrary", "arbitrary"),
        name="branch_d_rglru",
    )(z, z, conv_w, conv_b.reshape(1, -1), wa_bd_bf, ba.reshape(1, -1), wx_bd_bf, bx.reshape(1, -1),
      lam.reshape(1, -1))


def _merge_kernel(ya_ref, yb_ref, yc_ref, yd_ref, g0_ref, g1_ref, g2_ref, g3_ref, h_ref, wb_ref, wo_ref,
                  bo_ref, lg_ref, lb_ref, o_ref, *, alpha):
    merged = None
    for n, (y_ref, g_ref) in enumerate(((ya_ref, g0_ref), (yb_ref, g1_ref), (yc_ref, g2_ref), (yd_ref, g3_ref))):
        term = _sigmoid(g_ref[...]) * _dot(y_ref[...].astype(BF16), wb_ref[n])
        merged = term if merged is None else merged + term
    mix = _dot(merged.astype(BF16), wo_ref[...]) + bo_ref[...]
    o_ref[...] = _ln(alpha * h_ref[...] + mix, lg_ref[...], lb_ref[...])


def _merge(ya, yb, yc, yd, z, h, w_branch_bf, w_out_bf, b_out, ln_g, ln_b, alpha, tm=256):
    t, d = h.shape
    ysp = lambda: pl.BlockSpec((tm, MIX_W), lambda i: (i, 0))
    gsp = lambda n: pl.BlockSpec((tm, d), lambda i: (i, COL_G // d + n))
    vec = lambda: pl.BlockSpec((1, d), lambda i: (0, 0))
    return pl.pallas_call(
        functools.partial(_merge_kernel, alpha=alpha),
        grid=(t // tm,),
        in_specs=[ysp(), ysp(), ysp(), ysp(), gsp(0), gsp(1), gsp(2), gsp(3),
                  pl.BlockSpec((tm, d), lambda i: (i, 0)),
                  pl.BlockSpec((4, MIX_W, d), lambda i: (0, 0, 0)),
                  pl.BlockSpec((d, d), lambda i: (0, 0)),
                  vec(), vec(), vec()],
        out_specs=pl.BlockSpec((tm, d), lambda i: (i, 0)),
        out_shape=jax.ShapeDtypeStruct((t, d), F32),
        compiler_params=_cparams("arbitrary"),
        name="merge_out_ln",
    )(ya, yb, yc, yd, z, z, z, z, h, w_branch_bf, w_out_bf, b_out.reshape(1, d), ln_g.reshape(1, d),
      ln_b.reshape(1, d))


def _xattn_kernel(h_ref, wq_ref, kv_ref, wo_ref, lg_ref, lb_ref, o_ref, obuf, *, alpha, d):
    h = h_ref[...]
    q = _dot(h.astype(BF16), wq_ref[...])
    dh = d // MEM_H
    scale = dh ** -0.5
    for hd in range(MEM_H):
        cs = slice(hd * dh, (hd + 1) * dh)
        k = kv_ref[:, cs].astype(BF16)
        v = kv_ref[:, d + hd * dh:d + (hd + 1) * dh].astype(BF16)
        s = _dot_nt(q[:, cs].astype(BF16), k) * scale
        s = s - jnp.max(s, axis=-1, keepdims=True)
        p = jnp.exp(s)
        p = p / jnp.sum(p, axis=-1, keepdims=True)
        obuf[:, cs] = _dot(p.astype(BF16), v)
    ca = _dot(obuf[...].astype(BF16), wo_ref[...])
    o_ref[...] = _ln(alpha * h + ca, lg_ref[...], lb_ref[...])


def _xattn(h, kv, wq_bf, wo_bf, ln_g, ln_b, alpha, seq, n_mem, tm=256):
    t, d = h.shape
    per_batch = seq // tm
    vec = lambda: pl.BlockSpec((1, d), lambda i: (0, 0))
    return pl.pallas_call(
        functools.partial(_xattn_kernel, alpha=alpha, d=d),
        grid=(t // tm,),
        in_specs=[pl.BlockSpec((tm, d), lambda i: (i, 0)),
                  pl.BlockSpec((d, d), lambda i: (0, 0)),
                  pl.BlockSpec((n_mem, 2 * d), lambda i: (i // per_batch, 0)),
                  pl.BlockSpec((d, d), lambda i: (0, 0)),
                  vec(), vec()],
        out_specs=pl.BlockSpec((tm, d), lambda i: (i, 0)),
        out_shape=jax.ShapeDtypeStruct((t, d), F32),
        scratch_shapes=[pltpu.VMEM((tm, d), F32)],
        compiler_params=_cparams("arbitrary"),
        name="xattn_ln",
    )(h, wq_bf, kv, wo_bf, ln_g.reshape(1, d), ln_b.reshape(1, d))


ROUTE_IDX, ROUTE_GATE, ROUTE_RANK = 0, 4, 8


def _router_kernel(h_ref, whi_ref, wlo_ref, b_ref, lt_ref, route_ref, cnt_ref, cnt):
    @pl.when(pl.program_id(0) == 0)
    def _():
        cnt[...] = jnp.zeros_like(cnt)

    x_hi, x_lo = _split_bf16(h_ref[...])
    logits = _dot(x_hi, whi_ref[...]) + _dot(x_lo, whi_ref[...]) + _dot(x_hi, wlo_ref[...]) + b_ref[...]
    tm = logits.shape[0]
    lane = lax.broadcasted_iota(jnp.int32, (tm, LANES), 1).astype(F32)
    cur = logits
    vals, idxs = [], []
    for _ in range(TOP_K):
        m = jnp.max(cur, axis=-1, keepdims=True)
        idx = jnp.min(jnp.where(cur == m, lane, float(LANES)), axis=-1, keepdims=True)
        vals.append(m)
        idxs.append(idx)
        cur = jnp.where(lane == idx, -jnp.inf, cur)
    ex = [jnp.exp(v - vals[0]) for v in vals]
    denom = ex[0] + ex[1] + ex[2] + ex[3]
    onehot = jnp.zeros((tm, LANES), F32)
    for idx in idxs:
        onehot = onehot + jnp.where(lane == idx, 1.0, 0.0)
    before = _dot(lt_ref[...], onehot.astype(BF16)) + cnt[...]
    route = jnp.zeros((tm, LANES), F32)
    for k in range(TOP_K):
        rank = jnp.sum(jnp.where(lane == idxs[k], before, 0.0), axis=-1, keepdims=True)
        route = jnp.where(lane == float(ROUTE_IDX + k), idxs[k], route)
        route = jnp.where(lane == float(ROUTE_GATE + k), ex[k] / denom, route)
        route = jnp.where(lane == float(ROUTE_RANK + k), rank, route)
    route_ref[...] = route
    cnt[...] = cnt[...] + jnp.sum(onehot, axis=0, keepdims=True)
    cnt_ref[...] = cnt[...]


def _router(h, router_w, router_b, tm=256):
    t, d = h.shape
    w_pad = jnp.zeros((d, LANES), F32).at[:, :N_EXPERTS].set(router_w)
    w_hi = w_pad.astype(BF16)
    w_lo = (w_pad - w_hi.astype(F32)).astype(BF16)
    b_pad = jnp.full((1, LANES), -1e30, F32).at[0, :N_EXPERTS].set(router_b)
    row = lax.broadcasted_iota(jnp.int32, (tm, tm), 0)
    col = lax.broadcasted_iota(jnp.int32, (tm, tm), 1)
    lower = jnp.where(col < row, 1.0, 0.0).astype(BF16)
    return pl.pallas_call(
        _router_kernel,
        grid=(t // tm,),
        in_specs=[pl.BlockSpec((tm, d), lambda i: (i, 0)),
                  pl.BlockSpec((d, LANES), lambda i: (0, 0)),
                  pl.BlockSpec((d, LANES), lambda i: (0, 0)),
                  pl.BlockSpec((1, LANES), lambda i: (0, 0)),
                  pl.BlockSpec((tm, tm), lambda i: (0, 0))],
        out_specs=[pl.BlockSpec((tm, LANES), lambda i: (i, 0)),
                   pl.BlockSpec((1, LANES), lambda i: (0, 0))],
        out_shape=[jax.ShapeDtypeStruct((t, LANES), F32), jax.ShapeDtypeStruct((1, LANES), F32)],
        scratch_shapes=[pltpu.VMEM((1, LANES), F32)],
        compiler_params=_cparams("arbitrary"),
        name="moe_router",
    )(h, w_hi, w_lo, b_pad, lower)


def _expert_kernel(be_ref, nu_ref, tok_ref, x_hbm, w1_ref, b1_ref, w2_ref, b2_ref, o_ref, xbuf, w1b, w2b, sems, *, ff):
    i = pl.program_id(0)
    n_used = nu_ref[0]

    def row_copy(blk, slot, r):
        return pltpu.make_async_copy(x_hbm.at[pl.ds(tok_ref[blk * MOE_BLOCK + r], 1)],
                                     xbuf.at[slot, pl.ds(r, 1)], sems.at[slot])

    def issue(blk, slot):
        def body(r, _):
            row_copy(blk, slot, r).start()
            return 0
        lax.fori_loop(0, MOE_BLOCK, body, 0, unroll=8)

    def drain(slot):
        pltpu.make_async_copy(x_hbm.at[pl.ds(0, MOE_BLOCK)], xbuf.at[slot], sems.at[slot]).wait()

    def ffn(slot, prefetch):
        x = xbuf[slot].astype(BF16)
        if prefetch:
            for r in range(MOE_BLOCK):
                row_copy(i + 1, 1 - slot, r).start()
        hcat = _dot(x, w1b[...]) + b1_ref[0]
        g = jnp.minimum(hcat[:, :ff], SWIGLU_LIMIT)
        lin = jnp.clip(hcat[:, ff:], -SWIGLU_LIMIT, SWIGLU_LIMIT)
        act = g * _sigmoid(SWIGLU_ALPHA * g) * (lin + 1.0)
        o_ref[...] = _dot(act.astype(BF16), w2b[...]) + b2_ref[0]

    @pl.when(i < n_used)
    def _():
        slot = lax.rem(i, 2)

        @pl.when(i == 0)
        def _():
            issue(0, 0)

        @pl.when(jnp.logical_or(i == 0, be_ref[i] != be_ref[jnp.maximum(i - 1, 0)]))
        def _():
            w1b[...] = w1_ref[0].astype(BF16)
            w2b[...] = w2_ref[0].astype(BF16)

        drain(slot)

        @pl.when(i + 1 < n_used)
        def _():
            ffn(slot, True)

        @pl.when(i + 1 >= n_used)
        def _():
            ffn(slot, False)

    @pl.when(i >= n_used)
    def _():
        o_ref[...] = jnp.zeros_like(o_ref)


def _experts(h, slot_tok, block_e, n_used, w1, b1, w2, b2):
    t, d = h.shape
    ff = w2.shape[1]
    n_slots = slot_tok.shape[0]
    n_blocks = n_slots // MOE_BLOCK
    return pl.pallas_call(
        functools.partial(_expert_kernel, ff=ff),
        grid_spec=pltpu.PrefetchScalarGridSpec(
            num_scalar_prefetch=3,
            grid=(n_blocks,),
            in_specs=[pl.BlockSpec(memory_space=pl.ANY),
                      pl.BlockSpec((1, d, 2 * ff), lambda i, be, nu, tok: (be[i], 0, 0)),
                      pl.BlockSpec((1, 1, 2 * ff), lambda i, be, nu, tok: (be[i], 0, 0)),
                      pl.BlockSpec((1, ff, d), lambda i, be, nu, tok: (be[i], 0, 0)),
                      pl.BlockSpec((1, 1, d), lambda i, be, nu, tok: (be[i], 0, 0))],
            out_specs=pl.BlockSpec((MOE_BLOCK, d), lambda i, be, nu, tok: (i, 0)),
            scratch_shapes=[pltpu.VMEM((2, MOE_BLOCK, d), F32), pltpu.VMEM((d, 2 * ff), BF16),
                            pltpu.VMEM((ff, d), BF16), pltpu.SemaphoreType.DMA((2,))],
        ),
        out_shape=jax.ShapeDtypeStruct((n_slots, d), F32),
        compiler_params=_cparams("arbitrary"),
        name="moe_experts",
    )(block_e, n_used, slot_tok, h, w1, b1.reshape(N_EXPERTS, 1, 2 * ff), w2, b2.reshape(N_EXPERTS, 1, d))


COMBINE_TOKENS = 128


def _combine_kernel(dest_ref, ys_hbm, route_ref, h_ref, lg_ref, lb_ref, o_ref, buf, sems, *, alpha):
    i = pl.program_id(0)
    n = pl.num_programs(0)
    tm = COMBINE_TOKENS

    def row_copy(step, slot, t, k):
        src = dest_ref[(step * tm + t) * TOP_K + k]
        return pltpu.make_async_copy(ys_hbm.at[pl.ds(src, 1)], buf.at[slot, k, pl.ds(t, 1)], sems.at[slot])

    slot = lax.rem(i, 2)

    @pl.when(i == 0)
    def _():
        def body(t, _):
            for k in range(TOP_K):
                row_copy(0, 0, t, k).start()
            return 0
        lax.fori_loop(0, tm, body, 0)

    @pl.when(i + 1 < n)
    def _():
        for t in range(tm):
            for k in range(TOP_K):
                row_copy(i + 1, 1 - slot, t, k).start()

    for k in range(TOP_K):
        pltpu.make_async_copy(ys_hbm.at[pl.ds(0, tm)], buf.at[slot, k], sems.at[slot]).wait()
    route = route_ref[...]
    ff = jnp.zeros(o_ref.shape, F32)
    for k in range(TOP_K):
        ff = ff + route[:, ROUTE_GATE + k:ROUTE_GATE + k + 1] * buf[slot, k]
    o_ref[...] = _ln(alpha * h_ref[...] + ff, lg_ref[...], lb_ref[...])


def _combine(ys, dest_flat, route, h, ln_g, ln_b, alpha):
    t, d = h.shape
    tm = COMBINE_TOKENS
    vec = lambda: pl.BlockSpec((1, d), lambda i, dst: (0, 0))
    return pl.pallas_call(
        functools.partial(_combine_kernel, alpha=alpha),
        grid_spec=pltpu.PrefetchScalarGridSpec(
            num_scalar_prefetch=1,
            grid=(t // tm,),
            in_specs=[pl.BlockSpec(memory_space=pl.ANY),
                      pl.BlockSpec((tm, LANES), lambda i, dst: (i, 0)),
                      pl.BlockSpec((tm, d), lambda i, dst: (i, 0)),
                      vec(), vec()],
            out_specs=pl.BlockSpec((tm, d), lambda i, dst: (i, 0)),
            scratch_shapes=[pltpu.VMEM((2, TOP_K, tm, d), F32), pltpu.SemaphoreType.DMA((2,))],
        ),
        out_shape=jax.ShapeDtypeStruct((t, d), F32),
        compiler_params=_cparams("arbitrary"),
        name="moe_combine_ln",
    )(dest_flat, ys, route, h, ln_g.reshape(1, d), ln_b.reshape(1, d))


def _moe(h, router_w, router_b, w1, b1, w2, b2, ln_g, ln_b, alpha):
    t, d = h.shape
    route, cnt = _router(h, router_w, router_b)
    idx = route[:, ROUTE_IDX:ROUTE_IDX + TOP_K].astype(jnp.int32)
    rank = route[:, ROUTE_RANK:ROUTE_RANK + TOP_K].astype(jnp.int32)
    counts = cnt[0, :N_EXPERTS].astype(jnp.int32)
    padded = ((counts + MOE_BLOCK - 1) // MOE_BLOCK) * MOE_BLOCK
    pend = jnp.cumsum(padded)
    pstart = pend - padded
    dest = (pstart[idx] + rank).reshape(t * TOP_K)
    n_blocks = t * TOP_K // MOE_BLOCK + N_EXPERTS
    n_used = (pend[-1:] // MOE_BLOCK).astype(jnp.int32)
    block_start = jnp.arange(n_blocks, dtype=jnp.int32) * MOE_BLOCK
    block_e = jnp.minimum(jnp.sum((pend[None, :] <= block_start[:, None]).astype(jnp.int32), axis=1), N_EXPERTS - 1)
    tok = jnp.arange(t * TOP_K, dtype=jnp.int32) // TOP_K
    slot_tok = jnp.zeros((n_blocks * MOE_BLOCK,), jnp.int32).at[dest].set(tok, unique_indices=True)
    ys = _experts(h, slot_tok, block_e, n_used, w1, b1, w2, b2)
    return _combine(ys, dest, route, h, ln_g, ln_b, alpha)


def _block_diag(w):
    n, bw, _ = w.shape
    eye = jnp.eye(n, dtype=w.dtype)
    return (eye[:, None, :, None] * w[:, :, None, :]).reshape(n * bw, n * bw)


def kernel(x, mem, ln0_g, ln0_b, w_in, b_in, conv_a_w, conv_a_b, ln_a_g, ln_a_b, gla_wa2, gla_ba, gla_norm_g, conv_d_w, conv_d_b, lru_wa, lru_ba, lru_wx, lru_bx, lru_lambda, w_branch, w_out, b_out, ln1_g, ln1_b, ca_wq, ca_wk, ca_wv, ca_wo, ln2_g, ln2_b, router_w, router_b, moe_w1, moe_b1, moe_w2, moe_b2, ln3_g, ln3_b):
    batch, seq, d = x.shape
    n_mem = mem.shape[1]
    depth = w_in.shape[0]
    alpha = (2 * depth) ** 0.25
    t = batch * seq
    h = _layer_norm(x.reshape(t, d), ln0_g, ln0_b)
    mem2 = mem.reshape(batch * n_mem, d)
    for l in range(depth):
        w_main = jnp.concatenate([w_in[l][:, :LR_START], w_in[l][:, LR_END:]], axis=1).astype(BF16)
        b_main = jnp.concatenate([b_in[l][:LR_START], b_in[l][LR_END:]])
        w_lr = jnp.zeros((d, LANES), F32).at[:, :GLA_LOWRANK].set(w_in[l][:, LR_START:LR_END]).astype(BF16)
        b_lr = jnp.zeros((LANES,), F32).at[:GLA_LOWRANK].set(b_in[l][LR_START:LR_END])
        z = _matmul_bias(h, w_main, b_main, tm=1024, tn=512)
        zlr = _matmul_bias(h, w_lr, b_lr, tm=1024, tn=LANES)
        wa2_pad = jnp.zeros((LANES, GLA_H * GLA_DK), F32).at[:GLA_LOWRANK].set(gla_wa2[l]).astype(BF16)
        ya = _branch_a(z, conv_a_w[l], conv_a_b[l], ln_a_g[l], ln_a_b[l], batch, seq)
        yb = _branch_b(z, zlr, wa2_pad, gla_ba[l], gla_norm_g[l], batch, seq)
        yc = _branch_c(z, batch, seq)
        yd = _branch_d(z, conv_d_w[l], conv_d_b[l], _block_diag(lru_wa[l]).astype(BF16), lru_ba[l],
                       _block_diag(lru_wx[l]).astype(BF16), lru_bx[l], lru_lambda[l], batch, seq)
        h = _merge(ya, yb, yc, yd, z, h, w_branch[l].astype(BF16), w_out[l].astype(BF16), b_out[l],
                   ln1_g[l], ln1_b[l], alpha)
        w_kv = jnp.concatenate([ca_wk[l], ca_wv[l]], axis=1).astype(BF16)
        kv = _matmul_bias(mem2, w_kv, jnp.zeros((2 * d,), F32), tm=batch * n_mem, tn=512)
        h = _xattn(h, kv, ca_wq[l].astype(BF16), ca_wo[l].astype(BF16), ln2_g[l], ln2_b[l], alpha, seq, n_mem)
        h = _moe(h, router_w[l], router_b[l], moe_w1[l], moe_b1[l], moe_w2[l], moe_b2[l], ln3_g[l], ln3_b[l], alpha)
    return h.reshape(batch, seq, d)
```

```python
import functools

import jax
import jax.numpy as jnp
from jax import lax
from jax.experimental import pallas as pl
from jax.experimental.pallas import tpu as pltpu

F32 = jnp.float32
BF16 = jnp.bfloat16

MIX_W = 512
CONV_A_WIDTH = 31
GLA_H, GLA_DK, GLA_DV = 4, 64, 128
GLA_LOWRANK = 16
GLA_TAU = 16.0
GLA_CHUNK = 64
SB_H, SB_DH, SB_BLOCK = 8, 64, 128
LRU_C = 8.0
LRU_CONV = 4
MEM_H = 4
N_EXPERTS, TOP_K = 32, 4
MOE_BLOCK = 128
SWIGLU_LIMIT = 7.0
SWIGLU_ALPHA = 1.702
LN_EPS = 1e-5
LANES = 128
VMEM_LIMIT = 48 * 1024 * 1024

COL_A_VAL, COL_A_GATE = 0, 512
COL_B_Q, COL_B_K, COL_B_V, COL_B_R = 1024, 1280, 1536, 2048
COL_C_Q, COL_C_K, COL_C_V = 2560, 3072, 3584
COL_D_X, COL_D_G = 4096, 4608
COL_G = 5120
Z_COLS = 9216
LR_START, LR_END = 2560, 2576


def _cparams(*sem):
    return pltpu.CompilerParams(dimension_semantics=sem, vmem_limit_bytes=VMEM_LIMIT)


def _ln(x, g, b):
    mu = jnp.mean(x, axis=-1, keepdims=True)
    xc = x - mu
    var = jnp.mean(xc * xc, axis=-1, keepdims=True)
    return xc * lax.rsqrt(var + LN_EPS) * g + b


def _sigmoid(x):
    return 1.0 / (1.0 + jnp.exp(-x))


def _softplus(x):
    return jnp.maximum(x, 0.0) + jnp.log1p(jnp.exp(-jnp.abs(x)))


def _split_bf16(x):
    hi = x.astype(BF16)
    lo = (x - hi.astype(F32)).astype(BF16)
    return hi, lo


def _dot(a, b):
    return jnp.dot(a, b, preferred_element_type=F32)


def _dot_nt(a, b):
    return lax.dot_general(a, b, (((1,), (1,)), ((), ())), preferred_element_type=F32)


def _dot_tn(a, b):
    return lax.dot_general(a, b, (((0,), (0,)), ((), ())), preferred_element_type=F32)


def _ln_kernel(x_ref, g_ref, b_ref, o_ref):
    o_ref[...] = _ln(x_ref[...], g_ref[...], b_ref[...])


def _layer_norm(x, g, b, tm=512):
    t, d = x.shape
    return pl.pallas_call(
        _ln_kernel,
        grid=(t // tm,),
        in_specs=[pl.BlockSpec((tm, d), lambda i: (i, 0)),
                  pl.BlockSpec((1, d), lambda i: (0, 0)),
                  pl.BlockSpec((1, d), lambda i: (0, 0))],
        out_specs=pl.BlockSpec((tm, d), lambda i: (i, 0)),
        out_shape=jax.ShapeDtypeStruct((t, d), F32),
        compiler_params=_cparams("arbitrary"),
        name="layer_norm",
    )(x, g.reshape(1, d), b.reshape(1, d))


def _mm_kernel(a_ref, w_ref, b_ref, o_ref, abf_ref):
    @pl.when(pl.program_id(1) == 0)
    def _():
        abf_ref[...] = a_ref[...].astype(BF16)

    o_ref[...] = _dot(abf_ref[...], w_ref[...]) + b_ref[...]


def _matmul_bias(a, w_bf, b, tm, tn):
    m, k = a.shape
    n = w_bf.shape[1]
    return pl.pallas_call(
        _mm_kernel,
        grid=(m // tm, n // tn),
        in_specs=[pl.BlockSpec((tm, k), lambda i, j: (i, 0)),
                  pl.BlockSpec((k, tn), lambda i, j: (0, j)),
                  pl.BlockSpec((1, tn), lambda i, j: (0, j))],
        out_specs=pl.BlockSpec((tm, tn), lambda i, j: (i, j)),
        out_shape=jax.ShapeDtypeStruct((m, n), F32),
        scratch_shapes=[pltpu.VMEM((tm, k), BF16)],
        compiler_params=_cparams("arbitrary", "arbitrary"),
        name="matmul_bias",
    )(a, w_bf, b.reshape(1, n))


PACK_TILE = 512


def _repack_kernel(a_ref, b_ref, o_ref):
    j = pl.program_id(1)
    lr = LR_END - LR_START

    @pl.when(j < LR_START // PACK_TILE)
    def _():
        o_ref[0] = a_ref[0].astype(BF16)

    @pl.when(j >= LR_START // PACK_TILE)
    def _():
        x = jnp.concatenate([a_ref[0], b_ref[0]], axis=1)
        o_ref[0] = pltpu.roll(x, x.shape[1] - lr, axis=1)[:, :PACK_TILE].astype(BF16)


def _repack_w_in(w_in):
    depth, d, _ = w_in.shape
    assert LR_START % PACK_TILE == 0 and Z_COLS % PACK_TILE == 0
    return pl.pallas_call(
        _repack_kernel,
        grid=(depth, Z_COLS // PACK_TILE),
        in_specs=[pl.BlockSpec((1, d, PACK_TILE), lambda l, j: (l, 0, j)),
                  pl.BlockSpec((1, d, LANES), lambda l, j: (l, 0, (j + 1) * (PACK_TILE // LANES)))],
        out_specs=pl.BlockSpec((1, d, PACK_TILE), lambda l, j: (l, 0, j)),
        out_shape=jax.ShapeDtypeStruct((depth, d, Z_COLS), BF16),
        compiler_params=_cparams("arbitrary", "arbitrary"),
        name="repack_w_in",
    )(w_in, w_in)


A_HALO = 32
A_ROWS = 64


def _conv_a_kernel(val_ref, gate_ref, w_ref, cb_ref, g_ref, b_ref, o_ref, ubuf, ybuf, *, tt):
    i = pl.program_id(1)

    @pl.when(i == 0)
    def _():
        ubuf[0:A_HALO, :] = jnp.zeros((A_HALO, MIX_W), F32)

    @pl.when(i > 0)
    def _():
        ubuf[0:A_HALO, :] = ubuf[tt:tt + A_HALO, :]

    ubuf[A_HALO:A_HALO + tt, :] = val_ref[...] * _sigmoid(gate_ref[...])
    off = A_HALO - (CONV_A_WIDTH - 1)
    for r0 in range(0, tt, A_ROWS):
        for c0 in range(0, MIX_W, LANES):
            acc = jnp.zeros((A_ROWS, LANES), F32) + cb_ref[:, c0:c0 + LANES]
            for j in range(CONV_A_WIDTH):
                acc = acc + w_ref[j:j + 1, c0:c0 + LANES] * ubuf[r0 + off + j:r0 + off + j + A_ROWS, c0:c0 + LANES]
            ybuf[r0:r0 + A_ROWS, c0:c0 + LANES] = acc
    y = _ln(ybuf[...], g_ref[...], b_ref[...])
    o_ref[...] = y * _sigmoid(y)


def _branch_a(z, conv_w, conv_b, ln_g, ln_b, batch, seq, tt=256):
    nt = seq // tt
    vec = lambda: pl.BlockSpec((1, MIX_W), lambda b, i: (0, 0))
    return pl.pallas_call(
        functools.partial(_conv_a_kernel, tt=tt),
        grid=(batch, nt),
        in_specs=[pl.BlockSpec((tt, MIX_W), lambda b, i: (b * nt + i, COL_A_VAL // MIX_W)),
                  pl.BlockSpec((tt, MIX_W), lambda b, i: (b * nt + i, COL_A_GATE // MIX_W)),
                  pl.BlockSpec((CONV_A_WIDTH, MIX_W), lambda b, i: (0, 0)),
                  vec(), vec(), vec()],
        out_specs=pl.BlockSpec((tt, MIX_W), lambda b, i: (b * nt + i, 0)),
        out_shape=jax.ShapeDtypeStruct((batch * seq, MIX_W), F32),
        scratch_shapes=[pltpu.VMEM((A_HALO + tt, MIX_W), F32), pltpu.VMEM((tt, MIX_W), F32)],
        compiler_params=_cparams("arbitrary", "arbitrary"),
        name="branch_a_conv",
    )(z, z, conv_w, conv_b.reshape(1, -1), ln_g.reshape(1, -1), ln_b.reshape(1, -1))


def _gla_kernel(q_ref, k_ref, v_ref, r_ref, lr_ref, wa2_ref, ba_ref, g_ref, o_ref, state, *, tt):
    @pl.when(pl.program_id(1) == 0)
    def _():
        state[...] = jnp.zeros_like(state)

    c = GLA_CHUNK
    row = lax.broadcasted_iota(jnp.int32, (c, c), 0)
    col = lax.broadcasted_iota(jnp.int32, (c, c), 1)
    causal = col <= row
    tri = jnp.where(causal, 1.0, 0.0).astype(BF16)
    ones = jnp.ones((c, LANES), BF16)
    scale = GLA_DK ** -0.5
    for c0 in range(0, tt, c):
        rows = slice(c0, c0 + c)
        x = _dot(lr_ref[rows, :].astype(BF16), wa2_ref[...]) + ba_ref[...]
        la = -_softplus(-x) / GLA_TAU
        la_hi, la_lo = _split_bf16(la)
        b = _dot(tri, la_hi) + _dot(tri, la_lo)
        b_last = b[c - 1:c, :]
        q_dec = (q_ref[rows, :] * scale) * jnp.exp(b)
        k = k_ref[rows, :]
        k_inv = (k * jnp.exp(-b)).astype(BF16)
        k_end = (k * jnp.exp(b_last - b)).astype(BF16)
        q_dec = q_dec.astype(BF16)
        dsum = _dot_tn(la_hi, ones) + _dot_tn(la_lo, ones)
        for h in range(GLA_H):
            ks = slice(h * GLA_DK, (h + 1) * GLA_DK)
            vs = slice(h * GLA_DV, (h + 1) * GLA_DV)
            v = v_ref[rows, vs].astype(BF16)
            s = jnp.where(causal, _dot_nt(q_dec[:, ks], k_inv[:, ks]), 0.0)
            s_prev = state[h]
            o = _dot(s.astype(BF16), v) + _dot(q_dec[:, ks], s_prev.astype(BF16))
            state[h] = jnp.exp(dsum[ks, :]) * s_prev + _dot_tn(k_end[:, ks], v)
            o = o * lax.rsqrt(jnp.mean(o * o, axis=-1, keepdims=True) + LN_EPS) * g_ref[...]
            rg = r_ref[rows, vs]
            o_ref[rows, vs] = o * (rg * _sigmoid(rg))


def _branch_b(z, zlr, wa2_pad_bf, gla_ba, gla_norm_g, batch, seq, tt=256):
    nt = seq // tt
    hk = GLA_H * GLA_DK
    return pl.pallas_call(
        functools.partial(_gla_kernel, tt=tt),
        grid=(batch, nt),
        in_specs=[pl.BlockSpec((tt, hk), lambda b, i: (b * nt + i, COL_B_Q // hk)),
                  pl.BlockSpec((tt, hk), lambda b, i: (b * nt + i, COL_B_K // hk)),
                  pl.BlockSpec((tt, MIX_W), lambda b, i: (b * nt + i, COL_B_V // MIX_W)),
                  pl.BlockSpec((tt, MIX_W), lambda b, i: (b * nt + i, COL_B_R // MIX_W)),
                  pl.BlockSpec((tt, LANES), lambda b, i: (b * nt + i, 0)),
                  pl.BlockSpec((LANES, hk), lambda b, i: (0, 0)),
                  pl.BlockSpec((1, hk), lambda b, i: (0, 0)),
                  pl.BlockSpec((1, GLA_DV), lambda b, i: (0, 0))],
        out_specs=pl.BlockSpec((tt, MIX_W), lambda b, i: (b * nt + i, 0)),
        out_shape=jax.ShapeDtypeStruct((batch * seq, MIX_W), F32),
        scratch_shapes=[pltpu.VMEM((GLA_H, GLA_DK, GLA_DV), F32)],
        compiler_params=_cparams("arbitrary", "arbitrary"),
        name="branch_b_gla",
    )(z, z, z, z, zlr, wa2_pad_bf, gla_ba.reshape(1, -1), gla_norm_g.reshape(1, -1))


SB_PAIRS = 4
SB_QROWS = 256
SB_DEAD_LOG = -104.0


def _sb_kernel(q_ref, k_ref, v_ref, uj_ref, o_ref, kb, vb, ls_s, lk_s, w_s, carry_s, acc_s):
    qi = pl.program_id(2)
    blk = SB_BLOCK
    qr = SB_QROWS
    rows = pl.ds(pl.multiple_of(qi * qr, qr), qr)
    kb[rows, :] = k_ref[...].astype(BF16)
    vb[rows, :] = v_ref[...].astype(BF16)

    lane = lax.broadcasted_iota(jnp.int32, (qr, LANES), 1)
    row2 = lax.broadcasted_iota(jnp.int32, (2 * qr, blk), 0)
    col2 = lax.broadcasted_iota(jnp.int32, (2 * qr, blk), 1)
    ahead = jnp.bitwise_and(row2, qr - 1) - col2
    scale = SB_DH ** -0.5
    qs = []
    for p in range(SB_PAIRS):
        q = q_ref[:, p * LANES:(p + 1) * LANES] * scale
        qs.append(jnp.concatenate([jnp.where(lane < SB_DH, q, 0.0).astype(BF16),
                                   jnp.where(lane >= SB_DH, q, 0.0).astype(BF16)], axis=0))
        carry_s[p] = jnp.zeros((2 * qr, LANES), F32)
        acc_s[p] = jnp.zeros((2 * qr, LANES), F32)

    def tile_rows(j):
        return pl.ds(pl.multiple_of(j * blk, blk), blk)

    def stage_a(j, key_shift):
        for p in range(SB_PAIRS):
            ks = kb[tile_rows(j), p * LANES:(p + 1) * LANES]
            zz = _dot_nt(qs[p], ks)
            neg = -zz
            lk = jnp.minimum(neg, 0.0) - jnp.log(1.0 + jnp.exp(jnp.minimum(zz, neg)))
            ls = zz + lk
            if key_shift is not None:
                before = ahead > key_shift
                lk = jnp.where(before, lk, 0.0)
                ls = jnp.where(before, ls, -1e30)
            hi, lo = _split_bf16(lk)
            ls_s[p] = ls
            lk_s[p] = jnp.concatenate([hi, lo], axis=1)

    def stage_b():
        for p in range(SB_PAIRS):
            r = _dot(lk_s[p], uj_ref[...])
            carry = carry_s[p]
            w_s[p] = jnp.exp(ls_s[p] + carry + r[:, :LANES]).astype(BF16)
            carry_s[p] = carry + r[:, LANES:]

    def stage_c(j):
        for p in range(SB_PAIRS):
            acc_s[p] = acc_s[p] + _dot(w_s[p], vb[tile_rows(j), p * LANES:(p + 1) * LANES])

    tiles_per_q = qr // blk
    last = tiles_per_q * qi + tiles_per_q - 1
    stage_a(last, (tiles_per_q - 1) * blk)
    stage_b()
    stage_a(last - 1, (tiles_per_q - 2) * blk)

    def body(state):
        m, _ = state
        for mm in (m, m + 1):
            stage_c(last - mm + 2)
            stage_b()
            stage_a(last - mm, None)
        top = carry_s[0]
        for p in range(1, SB_PAIRS):
            top = jnp.maximum(top, carry_s[p])
        dead = (jnp.max(top) < SB_DEAD_LOG).astype(jnp.int32)
        return m + 2, dead

    m, dead = lax.while_loop(lambda st: jnp.logical_and(st[0] <= last, st[1] == 0), body,
                             (jnp.int32(2), jnp.int32(0)))
    stage_c(last - m + 2)

    @pl.when(dead == 0)
    def _():
        stage_b()
        stage_c(0)

    for p in range(SB_PAIRS):
        acc = acc_s[p]
        o_ref[:, p * LANES:(p + 1) * LANES] = jnp.where(lane < SB_DH, acc[:qr], acc[qr:])


def _branch_c(z, batch, seq):
    assert SB_QROWS == 2 * SB_BLOCK
    nq = seq // SB_QROWS
    width = SB_PAIRS * LANES
    groups = SB_H * SB_DH // width
    blk = SB_BLOCK
    row = lax.broadcasted_iota(jnp.int32, (blk, blk), 0)
    col = lax.broadcasted_iota(jnp.int32, (blk, blk), 1)
    u = jnp.where(row > col, 1.0, 0.0)
    uj = jnp.concatenate([u, jnp.ones((blk, blk), F32)], axis=1)
    uj = jnp.concatenate([uj, uj], axis=0).astype(BF16)
    blockspec = lambda col0: pl.BlockSpec((SB_QROWS, width), lambda b, g, i: (b * nq + i, col0 // width + g))
    stage = lambda lanes, dt: pltpu.VMEM((SB_PAIRS, 2 * SB_QROWS, lanes), dt)
    return pl.pallas_call(
        _sb_kernel,
        grid=(batch, groups, nq),
        in_specs=[blockspec(COL_C_Q), blockspec(COL_C_K), blockspec(COL_C_V),
                  pl.BlockSpec((2 * blk, 2 * blk), lambda b, g, i: (0, 0))],
        out_specs=pl.BlockSpec((SB_QROWS, width), lambda b, g, i: (b * nq + i, g)),
        out_shape=jax.ShapeDtypeStruct((batch * seq, MIX_W), F32),
        scratch_shapes=[pltpu.VMEM((seq, width), BF16), pltpu.VMEM((seq, width), BF16),
                        stage(LANES, F32), stage(2 * LANES, BF16), stage(LANES, BF16),
                        stage(LANES, F32), stage(LANES, F32)],
        compiler_params=_cparams("arbitrary", "arbitrary", "arbitrary"),
        name="branch_c_stick_breaking",
    )(z, z, z, uj)


D_HALO = 8


def _lru_kernel(x_ref, gate_ref, cw_ref, cb_ref, wa_ref, ba_ref, wx_ref, bx_ref, lam_ref, o_ref,
                xbuf, abuf, ubuf, hbuf, hprev, *, tt):
    i = pl.program_id(1)

    @pl.when(i == 0)
    def _():
        xbuf[0:D_HALO, :] = jnp.zeros((D_HALO, MIX_W), F32)
        hprev[...] = jnp.zeros_like(hprev)

    @pl.when(i > 0)
    def _():
        xbuf[0:D_HALO, :] = xbuf[tt:tt + D_HALO, :]

    xbuf[D_HALO:D_HALO + tt, :] = x_ref[...]
    off = D_HALO - (LRU_CONV - 1)
    xc = jnp.zeros((tt, MIX_W), F32) + cb_ref[...]
    for j in range(LRU_CONV):
        xc = xc + cw_ref[j:j + 1, :] * xbuf[off + j:off + j + tt, :]
    xc_bf = xc.astype(BF16)
    r = _sigmoid(_dot(xc_bf, wa_ref[...]) + ba_ref[...])
    gi = _sigmoid(_dot(xc_bf, wx_ref[...]) + bx_ref[...])
    log_a = LRU_C * r * (-_softplus(-lam_ref[...]))
    a = jnp.exp(log_a)
    abuf[...] = a
    ubuf[...] = jnp.sqrt(-jnp.tanh(log_a) * (a * a + 1.0)) * (gi * xc)

    def step(t, h):
        h = abuf[pl.ds(t, 1), :] * h + ubuf[pl.ds(t, 1), :]
        hbuf[pl.ds(t, 1), :] = h
        return h

    hprev[...] = lax.fori_loop(0, tt, step, hprev[...], unroll=8)
    g = gate_ref[...]
    gelu = 0.5 * g * (1.0 + jnp.tanh(0.7978845608028654 * (g + 0.044715 * g * g * g)))
    o_ref[...] = hbuf[...] * gelu


def _branch_d(z, conv_w, conv_b, wa_bd_bf, ba, wx_bd_bf, bx, lam, batch, seq, tt=256):
    nt = seq // tt
    vec = lambda: pl.BlockSpec((1, MIX_W), lambda b, i: (0, 0))
    mat = lambda: pl.BlockSpec((MIX_W, MIX_W), lambda b, i: (0, 0))
    return pl.pallas_call(
        functools.partial(_lru_kernel, tt=tt),
        grid=(batch, nt),
        in_specs=[pl.BlockSpec((tt, MIX_W), lambda b, i: (b * nt + i, COL_D_X // MIX_W)),
                  pl.BlockSpec((tt, MIX_W), lambda b, i: (b * nt + i, COL_D_G // MIX_W)),
                  pl.BlockSpec((LRU_CONV, MIX_W), lambda b, i: (0, 0)),
                  vec(), mat(), vec(), mat(), vec(), vec()],
        out_specs=pl.BlockSpec((tt, MIX_W), lambda b, i: (b * nt + i, 0)),
        out_shape=jax.ShapeDtypeStruct((batch * seq, MIX_W), F32),
        scratch_shapes=[pltpu.VMEM((D_HALO + tt, MIX_W), F32), pltpu.VMEM((tt, MIX_W), F32),
                        pltpu.VMEM((tt, MIX_W), F32), pltpu.VMEM((tt, MIX_W), F32),
                        pltpu.VMEM((1, MIX_W), F32)],
        compiler_params=_cparams("arbitrary", "arbitrary"),
        name="branch_d_rglru",
    )(z, z, conv_w, conv_b.reshape(1, -1), wa_bd_bf, ba.reshape(1, -1), wx_bd_bf, bx.reshape(1, -1),
      lam.reshape(1, -1))


def _merge_kernel(ya_ref, yb_ref, yc_ref, yd_ref, g0_ref, g1_ref, g2_ref, g3_ref, h_ref, wb_ref, wo_ref,
                  bo_ref, lg_ref, lb_ref, o_ref, *, alpha):
    merged = None
    for n, (y_ref, g_ref) in enumerate(((ya_ref, g0_ref), (yb_ref, g1_ref), (yc_ref, g2_ref), (yd_ref, g3_ref))):
        term = _sigmoid(g_ref[...]) * _dot(y_ref[...].astype(BF16), wb_ref[n])
        merged = term if merged is None else merged + term
    mix = _dot(merged.astype(BF16), wo_ref[...]) + bo_ref[...]
    o_ref[...] = _ln(alpha * h_ref[...] + mix, lg_ref[...], lb_ref[...])


def _merge(ya, yb, yc, yd, z, h, w_branch_bf, w_out_bf, b_out, ln_g, ln_b, alpha, tm=256):
    t, d = h.shape
    ysp = lambda: pl.BlockSpec((tm, MIX_W), lambda i: (i, 0))
    gsp = lambda n: pl.BlockSpec((tm, d), lambda i: (i, COL_G // d + n))
    vec = lambda: pl.BlockSpec((1, d), lambda i: (0, 0))
    return pl.pallas_call(
        functools.partial(_merge_kernel, alpha=alpha),
        grid=(t // tm,),
        in_specs=[ysp(), ysp(), ysp(), ysp(), gsp(0), gsp(1), gsp(2), gsp(3),
                  pl.BlockSpec((tm, d), lambda i: (i, 0)),
                  pl.BlockSpec((4, MIX_W, d), lambda i: (0, 0, 0)),
                  pl.BlockSpec((d, d), lambda i: (0, 0)),
                  vec(), vec(), vec()],
        out_specs=pl.BlockSpec((tm, d), lambda i: (i, 0)),
        out_shape=jax.ShapeDtypeStruct((t, d), F32),
        compiler_params=_cparams("arbitrary"),
        name="merge_out_ln",
    )(ya, yb, yc, yd, z, z, z, z, h, w_branch_bf, w_out_bf, b_out.reshape(1, d), ln_g.reshape(1, d),
      ln_b.reshape(1, d))


def _xattn_kernel(h_ref, wq_ref, kv_ref, wo_ref, lg_ref, lb_ref, o_ref, obuf, *, alpha, d):
    h = h_ref[...]
    q = _dot(h.astype(BF16), wq_ref[...])
    dh = d // MEM_H
    scale = dh ** -0.5
    for hd in range(MEM_H):
        cs = slice(hd * dh, (hd + 1) * dh)
        k = kv_ref[:, cs].astype(BF16)
        v = kv_ref[:, d + hd * dh:d + (hd + 1) * dh].astype(BF16)
        s = _dot_nt(q[:, cs].astype(BF16), k) * scale
        s = s - jnp.max(s, axis=-1, keepdims=True)
        p = jnp.exp(s)
        p = p / jnp.sum(p, axis=-1, keepdims=True)
        obuf[:, cs] = _dot(p.astype(BF16), v)
    ca = _dot(obuf[...].astype(BF16), wo_ref[...])
    o_ref[...] = _ln(alpha * h + ca, lg_ref[...], lb_ref[...])


def _xattn(h, kv, wq_bf, wo_bf, ln_g, ln_b, alpha, seq, n_mem, tm=256):
    t, d = h.shape
    per_batch = seq // tm
    vec = lambda: pl.BlockSpec((1, d), lambda i: (0, 0))
    return pl.pallas_call(
        functools.partial(_xattn_kernel, alpha=alpha, d=d),
        grid=(t // tm,),
        in_specs=[pl.BlockSpec((tm, d), lambda i: (i, 0)),
                  pl.BlockSpec((d, d), lambda i: (0, 0)),
                  pl.BlockSpec((n_mem, 2 * d), lambda i: (i // per_batch, 0)),
                  pl.BlockSpec((d, d), lambda i: (0, 0)),
                  vec(), vec()],
        out_specs=pl.BlockSpec((tm, d), lambda i: (i, 0)),
        out_shape=jax.ShapeDtypeStruct((t, d), F32),
        scratch_shapes=[pltpu.VMEM((tm, d), F32)],
        compiler_params=_cparams("arbitrary"),
        name="xattn_ln",
    )(h, wq_bf, kv, wo_bf, ln_g.reshape(1, d), ln_b.reshape(1, d))


ROUTE_IDX, ROUTE_GATE, ROUTE_RANK = 0, 4, 8


def _router_kernel(h_ref, whi_ref, wlo_ref, b_ref, lt_ref, route_ref, cnt_ref, cnt):
    @pl.when(pl.program_id(0) == 0)
    def _():
        cnt[...] = jnp.zeros_like(cnt)

    x_hi, x_lo = _split_bf16(h_ref[...])
    logits = _dot(x_hi, whi_ref[...]) + _dot(x_lo, whi_ref[...]) + _dot(x_hi, wlo_ref[...]) + b_ref[...]
    tm = logits.shape[0]
    lane = lax.broadcasted_iota(jnp.int32, (tm, LANES), 1).astype(F32)
    cur = logits
    vals, idxs = [], []
    for _ in range(TOP_K):
        m = jnp.max(cur, axis=-1, keepdims=True)
        idx = jnp.min(jnp.where(cur == m, lane, float(LANES)), axis=-1, keepdims=True)
        vals.append(m)
        idxs.append(idx)
        cur = jnp.where(lane == idx, -jnp.inf, cur)
    ex = [jnp.exp(v - vals[0]) for v in vals]
    denom = ex[0] + ex[1] + ex[2] + ex[3]
    onehot = jnp.zeros((tm, LANES), F32)
    for idx in idxs:
        onehot = onehot + jnp.where(lane == idx, 1.0, 0.0)
    before = _dot(lt_ref[...], onehot.astype(BF16)) + cnt[...]
    route = jnp.zeros((tm, LANES), F32)
    for k in range(TOP_K):
        rank = jnp.sum(jnp.where(lane == idxs[k], before, 0.0), axis=-1, keepdims=True)
        route = jnp.where(lane == float(ROUTE_IDX + k), idxs[k], route)
        route = jnp.where(lane == float(ROUTE_GATE + k), ex[k] / denom, route)
        route = jnp.where(lane == float(ROUTE_RANK + k), rank, route)
    route_ref[...] = route
    cnt[...] = cnt[...] + jnp.sum(onehot, axis=0, keepdims=True)
    cnt_ref[...] = cnt[...]


def _router(h, router_w, router_b, tm=256):
    t, d = h.shape
    w_pad = jnp.zeros((d, LANES), F32).at[:, :N_EXPERTS].set(router_w)
    w_hi = w_pad.astype(BF16)
    w_lo = (w_pad - w_hi.astype(F32)).astype(BF16)
    b_pad = jnp.full((1, LANES), -1e30, F32).at[0, :N_EXPERTS].set(router_b)
    row = lax.broadcasted_iota(jnp.int32, (tm, tm), 0)
    col = lax.broadcasted_iota(jnp.int32, (tm, tm), 1)
    lower = jnp.where(col < row, 1.0, 0.0).astype(BF16)
    return pl.pallas_call(
        _router_kernel,
        grid=(t // tm,),
        in_specs=[pl.BlockSpec((tm, d), lambda i: (i, 0)),
                  pl.BlockSpec((d, LANES), lambda i: (0, 0)),
                  pl.BlockSpec((d, LANES), lambda i: (0, 0)),
                  pl.BlockSpec((1, LANES), lambda i: (0, 0)),
                  pl.BlockSpec((tm, tm), lambda i: (0, 0))],
        out_specs=[pl.BlockSpec((tm, LANES), lambda i: (i, 0)),
                   pl.BlockSpec((1, LANES), lambda i: (0, 0))],
        out_shape=[jax.ShapeDtypeStruct((t, LANES), F32), jax.ShapeDtypeStruct((1, LANES), F32)],
        scratch_shapes=[pltpu.VMEM((1, LANES), F32)],
        compiler_params=_cparams("arbitrary"),
        name="moe_router",
    )(h, w_hi, w_lo, b_pad, lower)


EXPERT_BUFS = 3


def _expert_kernel(be_ref, nu_ref, tok_ref, x_hbm, w1_ref, b1_ref, w2_ref, b2_ref, o_ref, xbuf, w1b, w2b, sems, *, ff):
    i = pl.program_id(0)
    n_used = nu_ref[0]

    def row_copy(blk, slot, r):
        return pltpu.make_async_copy(x_hbm.at[pl.ds(tok_ref[blk * MOE_BLOCK + r], 1)],
                                     xbuf.at[slot, pl.ds(r, 1)], sems.at[slot])

    def issue(blk, slot):
        def body(r, _):
            row_copy(blk, slot, r).start()
            return 0
        lax.fori_loop(0, MOE_BLOCK, body, 0, unroll=8)

    def drain(slot):
        pltpu.make_async_copy(x_hbm.at[pl.ds(0, MOE_BLOCK)], xbuf.at[slot], sems.at[slot]).wait()

    def ffn(slot, prefetch):
        x = xbuf[slot].astype(BF16)
        if prefetch:
            ahead = lax.rem(i + 2, EXPERT_BUFS)
            for r in range(MOE_BLOCK):
                row_copy(i + 2, ahead, r).start()
        hcat = _dot(x, w1b[...]) + b1_ref[0]
        g = jnp.minimum(hcat[:, :ff], SWIGLU_LIMIT)
        lin = jnp.clip(hcat[:, ff:], -SWIGLU_LIMIT, SWIGLU_LIMIT)
        act = g * _sigmoid(SWIGLU_ALPHA * g) * (lin + 1.0)
        o_ref[...] = _dot(act.astype(BF16), w2b[...]) + b2_ref[0]

    @pl.when(i < n_used)
    def _():
        slot = lax.rem(i, EXPERT_BUFS)

        @pl.when(i == 0)
        def _():
            issue(0, 0)

            @pl.when(n_used > 1)
            def _():
                issue(1, 1)

        @pl.when(jnp.logical_or(i == 0, be_ref[i] != be_ref[jnp.maximum(i - 1, 0)]))
        def _():
            w1b[...] = w1_ref[0].astype(BF16)
            w2b[...] = w2_ref[0].astype(BF16)

        drain(slot)

        @pl.when(i + 2 < n_used)
        def _():
            ffn(slot, True)

        @pl.when(i + 2 >= n_used)
        def _():
            ffn(slot, False)

    @pl.when(i >= n_used)
    def _():
        o_ref[...] = jnp.zeros_like(o_ref)


def _experts(h, slot_tok, block_e, n_used, w1, b1, w2, b2):
    t, d = h.shape
    ff = w2.shape[1]
    n_slots = slot_tok.shape[0]
    n_blocks = n_slots // MOE_BLOCK
    return pl.pallas_call(
        functools.partial(_expert_kernel, ff=ff),
        grid_spec=pltpu.PrefetchScalarGridSpec(
            num_scalar_prefetch=3,
            grid=(n_blocks,),
            in_specs=[pl.BlockSpec(memory_space=pl.ANY),
                      pl.BlockSpec((1, d, 2 * ff), lambda i, be, nu, tok: (be[i], 0, 0)),
                      pl.BlockSpec((1, 1, 2 * ff), lambda i, be, nu, tok: (be[i], 0, 0)),
                      pl.BlockSpec((1, ff, d), lambda i, be, nu, tok: (be[i], 0, 0)),
                      pl.BlockSpec((1, 1, d), lambda i, be, nu, tok: (be[i], 0, 0))],
            out_specs=pl.BlockSpec((MOE_BLOCK, d), lambda i, be, nu, tok: (i, 0)),
            scratch_shapes=[pltpu.VMEM((EXPERT_BUFS, MOE_BLOCK, d), F32), pltpu.VMEM((d, 2 * ff), BF16),
                            pltpu.VMEM((ff, d), BF16), pltpu.SemaphoreType.DMA((EXPERT_BUFS,))],
        ),
        out_shape=jax.ShapeDtypeStruct((n_slots, d), F32),
        compiler_params=_cparams("arbitrary"),
        name="moe_experts",
    )(block_e, n_used, slot_tok, h, w1, b1.reshape(N_EXPERTS, 1, 2 * ff), w2, b2.reshape(N_EXPERTS, 1, d))


COMBINE_TOKENS = 128


def _combine_kernel(dest_ref, ys_hbm, route_ref, h_ref, lg_ref, lb_ref, o_ref, buf, sems, *, alpha):
    i = pl.program_id(0)
    n = pl.num_programs(0)
    tm = COMBINE_TOKENS

    def row_copy(step, slot, t, k):
        src = dest_ref[(step * tm + t) * TOP_K + k]
        return pltpu.make_async_copy(ys_hbm.at[pl.ds(src, 1)], buf.at[slot, k, pl.ds(t, 1)], sems.at[slot])

    slot = lax.rem(i, 2)

    @pl.when(i == 0)
    def _():
        def body(t, _):
            for k in range(TOP_K):
                row_copy(0, 0, t, k).start()
            return 0
        lax.fori_loop(0, tm, body, 0)

    @pl.when(i + 1 < n)
    def _():
        for t in range(tm):
            for k in range(TOP_K):
                row_copy(i + 1, 1 - slot, t, k).start()

    for k in range(TOP_K):
        pltpu.make_async_copy(ys_hbm.at[pl.ds(0, tm)], buf.at[slot, k], sems.at[slot]).wait()
    route = route_ref[...]
    ff = jnp.zeros(o_ref.shape, F32)
    for k in range(TOP_K):
        ff = ff + route[:, ROUTE_GATE + k:ROUTE_GATE + k + 1] * buf[slot, k]
    o_ref[...] = _ln(alpha * h_ref[...] + ff, lg_ref[...], lb_ref[...])


def _combine(ys, dest_flat, route, h, ln_g, ln_b, alpha):
    t, d = h.shape
    tm = COMBINE_TOKENS
    vec = lambda: pl.BlockSpec((1, d), lambda i, dst: (0, 0))
    return pl.pallas_call(
        functools.partial(_combine_kernel, alpha=alpha),
        grid_spec=pltpu.PrefetchScalarGridSpec(
            num_scalar_prefetch=1,
            grid=(t // tm,),
            in_specs=[pl.BlockSpec(memory_space=pl.ANY),
                      pl.BlockSpec((tm, LANES), lambda i, dst: (i, 0)),
                      pl.BlockSpec((tm, d), lambda i, dst: (i, 0)),
                      vec(), vec()],
            out_specs=pl.BlockSpec((tm, d), lambda i, dst: (i, 0)),
            scratch_shapes=[pltpu.VMEM((2, TOP_K, tm, d), F32), pltpu.SemaphoreType.DMA((2,))],
        ),
        out_shape=jax.ShapeDtypeStruct((t, d), F32),
        compiler_params=_cparams("arbitrary"),
        name="moe_combine_ln",
    )(dest_flat, ys, route, h, ln_g.reshape(1, d), ln_b.reshape(1, d))


def _moe(h, router_w, router_b, w1, b1, w2, b2, ln_g, ln_b, alpha):
    t, d = h.shape
    route, cnt = _router(h, router_w, router_b)
    idx = route[:, ROUTE_IDX:ROUTE_IDX + TOP_K].astype(jnp.int32)
    rank = route[:, ROUTE_RANK:ROUTE_RANK + TOP_K].astype(jnp.int32)
    counts = cnt[0, :N_EXPERTS].astype(jnp.int32)
    padded = ((counts + MOE_BLOCK - 1) // MOE_BLOCK) * MOE_BLOCK
    pend = jnp.cumsum(padded)
    pstart = pend - padded
    dest = (pstart[idx] + rank).reshape(t * TOP_K)
    n_blocks = t * TOP_K // MOE_BLOCK + N_EXPERTS
    n_used = (pend[-1:] // MOE_BLOCK).astype(jnp.int32)
    block_start = jnp.arange(n_blocks, dtype=jnp.int32) * MOE_BLOCK
    block_e = jnp.minimum(jnp.sum((pend[None, :] <= block_start[:, None]).astype(jnp.int32), axis=1), N_EXPERTS - 1)
    tok = jnp.arange(t * TOP_K, dtype=jnp.int32) // TOP_K
    slot_tok = jnp.zeros((n_blocks * MOE_BLOCK,), jnp.int32).at[dest].set(tok, unique_indices=True)
    ys = _experts(h, slot_tok, block_e, n_used, w1, b1, w2, b2)
    return _combine(ys, dest, route, h, ln_g, ln_b, alpha)


def _block_diag(w):
    n, bw, _ = w.shape
    eye = jnp.eye(n, dtype=w.dtype)
    return (eye[:, None, :, None] * w[:, :, None, :]).reshape(n * bw, n * bw)


def kernel(x, mem, ln0_g, ln0_b, w_in, b_in, conv_a_w, conv_a_b, ln_a_g, ln_a_b, gla_wa2, gla_ba, gla_norm_g, conv_d_w, conv_d_b, lru_wa, lru_ba, lru_wx, lru_bx, lru_lambda, w_branch, w_out, b_out, ln1_g, ln1_b, ca_wq, ca_wk, ca_wv, ca_wo, ln2_g, ln2_b, router_w, router_b, moe_w1, moe_b1, moe_w2, moe_b2, ln3_g, ln3_b):
    batch, seq, d = x.shape
    n_mem = mem.shape[1]
    depth = w_in.shape[0]
    alpha = (2 * depth) ** 0.25
    t = batch * seq
    h = _layer_norm(x.reshape(t, d), ln0_g, ln0_b)
    mem2 = mem.reshape(batch * n_mem, d)
    w_main_all = _repack_w_in(w_in)
    for l in range(depth):
        w_main = w_main_all[l]
        b_main = jnp.concatenate([b_in[l][:LR_START], b_in[l][LR_END:]])
        w_lr = jnp.zeros((d, LANES), F32).at[:, :GLA_LOWRANK].set(w_in[l][:, LR_START:LR_END]).astype(BF16)
        b_lr = jnp.zeros((LANES,), F32).at[:GLA_LOWRANK].set(b_in[l][LR_START:LR_END])
        z = _matmul_bias(h, w_main, b_main, tm=1024, tn=512)
        zlr = _matmul_bias(h, w_lr, b_lr, tm=1024, tn=LANES)
        wa2_pad = jnp.zeros((LANES, GLA_H * GLA_DK), F32).at[:GLA_LOWRANK].set(gla_wa2[l]).astype(BF16)
        ya = _branch_a(z, conv_a_w[l], conv_a_b[l], ln_a_g[l], ln_a_b[l], batch, seq)
        yb = _branch_b(z, zlr, wa2_pad, gla_ba[l], gla_norm_g[l], batch, seq)
        yc = _branch_c(z, batch, seq)
        yd = _branch_d(z, conv_d_w[l], conv_d_b[l], _block_diag(lru_wa[l]).astype(BF16), lru_ba[l],
                       _block_diag(lru_wx[l]).astype(BF16), lru_bx[l], lru_lambda[l], batch, seq)
        h = _merge(ya, yb, yc, yd, z, h, w_branch[l].astype(BF16), w_out[l].astype(BF16), b_out[l],
                   ln1_g[l], ln1_b[l], alpha)
        w_kv = jnp.concatenate([ca_wk[l], ca_wv[l]], axis=1).astype(BF16)
        kv = _matmul_bias(mem2, w_kv, jnp.zeros((2 * d,), F32), tm=batch * n_mem, tn=512)
        h = _xattn(h, kv, ca_wq[l].astype(BF16), ca_wo[l].astype(BF16), ln2_g[l], ln2_b[l], alpha, seq, n_mem)
        h = _moe(h, router_w[l], router_b[l], moe_w1[l], moe_b1[l], moe_w2[l], moe_b2[l], ln3_g[l], ln3_b[l], alpha)
    return h.reshape(batch, seq, d)
```

```python
import functools

import jax
import jax.numpy as jnp
from jax import lax
from jax.experimental import pallas as pl
from jax.experimental.pallas import tpu as pltpu

F32 = jnp.float32
BF16 = jnp.bfloat16

MIX_W = 512
CONV_A_WIDTH = 31
GLA_H, GLA_DK, GLA_DV = 4, 64, 128
GLA_LOWRANK = 16
GLA_TAU = 16.0
GLA_CHUNK = 64
SB_H, SB_DH, SB_BLOCK = 8, 64, 128
LRU_C = 8.0
LRU_CONV = 4
MEM_H = 4
N_EXPERTS, TOP_K = 32, 4
MOE_BLOCK = 128
SWIGLU_LIMIT = 7.0
SWIGLU_ALPHA = 1.702
LN_EPS = 1e-5
LANES = 128
SUBLANES = 8
VMEM_LIMIT = 48 * 1024 * 1024

COL_A_VAL, COL_A_GATE = 0, 512
COL_B_Q, COL_B_K, COL_B_V, COL_B_R = 1024, 1280, 1536, 2048
COL_C_Q, COL_C_K, COL_C_V = 2560, 3072, 3584
COL_D_X, COL_D_G = 4096, 4608
COL_G = 5120
Z_COLS = 9216
LR_START, LR_END = 2560, 2576


def _cparams(*sem):
    return pltpu.CompilerParams(dimension_semantics=sem, vmem_limit_bytes=VMEM_LIMIT)


def _ln(x, g, b):
    mu = jnp.mean(x, axis=-1, keepdims=True)
    xc = x - mu
    var = jnp.mean(xc * xc, axis=-1, keepdims=True)
    return xc * lax.rsqrt(var + LN_EPS) * g + b


def _sigmoid(x):
    return 1.0 / (1.0 + jnp.exp(-x))


def _softplus(x):
    return jnp.maximum(x, 0.0) + jnp.log1p(jnp.exp(-jnp.abs(x)))


def _split_bf16(x):
    hi = x.astype(BF16)
    lo = (x - hi.astype(F32)).astype(BF16)
    return hi, lo


def _dot(a, b):
    return jnp.dot(a, b, preferred_element_type=F32)


def _dot_nt(a, b):
    return lax.dot_general(a, b, (((1,), (1,)), ((), ())), preferred_element_type=F32)


def _dot_tn(a, b):
    return lax.dot_general(a, b, (((0,), (0,)), ((), ())), preferred_element_type=F32)


def _ln_kernel(x_ref, g_ref, b_ref, o_ref):
    o_ref[...] = _ln(x_ref[...], g_ref[...], b_ref[...])


def _layer_norm(x, g, b, tm=512):
    t, d = x.shape
    return pl.pallas_call(
        _ln_kernel,
        grid=(t // tm,),
        in_specs=[pl.BlockSpec((tm, d), lambda i: (i, 0)),
                  pl.BlockSpec((1, d), lambda i: (0, 0)),
                  pl.BlockSpec((1, d), lambda i: (0, 0))],
        out_specs=pl.BlockSpec((tm, d), lambda i: (i, 0)),
        out_shape=jax.ShapeDtypeStruct((t, d), F32),
        compiler_params=_cparams("arbitrary"),
        name="layer_norm",
    )(x, g.reshape(1, d), b.reshape(1, d))


def _mm_kernel(a_ref, w_ref, b_ref, o_ref, abf_ref):
    @pl.when(pl.program_id(1) == 0)
    def _():
        abf_ref[...] = a_ref[...].astype(BF16)

    o_ref[...] = _dot(abf_ref[...], w_ref[0]) + b_ref[...]


def _matmul_bias(a, w_bf, b, tm, tn, layer=0):
    m, k = a.shape
    n = w_bf.shape[2]
    return pl.pallas_call(
        _mm_kernel,
        grid=(m // tm, n // tn),
        in_specs=[pl.BlockSpec((tm, k), lambda i, j: (i, 0)),
                  pl.BlockSpec((1, k, tn), lambda i, j: (layer, 0, j)),
                  pl.BlockSpec((1, tn), lambda i, j: (0, j))],
        out_specs=pl.BlockSpec((tm, tn), lambda i, j: (i, j)),
        out_shape=jax.ShapeDtypeStruct((m, n), F32),
        scratch_shapes=[pltpu.VMEM((tm, k), BF16)],
        compiler_params=_cparams("arbitrary", "arbitrary"),
        name="matmul_bias",
    )(a, w_bf, b.reshape(1, n))


PACK_TILE = 512


def _repack_kernel(a_ref, b_ref, o_ref):
    j = pl.program_id(1)
    lr = LR_END - LR_START

    @pl.when(j < LR_START // PACK_TILE)
    def _():
        o_ref[0] = a_ref[0].astype(BF16)

    @pl.when(j >= LR_START // PACK_TILE)
    def _():
        x = jnp.concatenate([a_ref[0], b_ref[0]], axis=1)
        o_ref[0] = pltpu.roll(x, x.shape[1] - lr, axis=1)[:, :PACK_TILE].astype(BF16)


def _repack_w_in(w_in):
    depth, d, _ = w_in.shape
    assert LR_START % PACK_TILE == 0 and Z_COLS % PACK_TILE == 0
    return pl.pallas_call(
        _repack_kernel,
        grid=(depth, Z_COLS // PACK_TILE),
        in_specs=[pl.BlockSpec((1, d, PACK_TILE), lambda l, j: (l, 0, j)),
                  pl.BlockSpec((1, d, LANES), lambda l, j: (l, 0, (j + 1) * (PACK_TILE // LANES)))],
        out_specs=pl.BlockSpec((1, d, PACK_TILE), lambda l, j: (l, 0, j)),
        out_shape=jax.ShapeDtypeStruct((depth, d, Z_COLS), BF16),
        compiler_params=_cparams("arbitrary", "arbitrary"),
        name="repack_w_in",
    )(w_in, w_in)


A_HALO = 32
A_ROWS = 64


def _conv_a_kernel(val_ref, gate_ref, w_ref, cb_ref, g_ref, b_ref, o_ref, ubuf, ush, ybuf, *, tt):
    i = pl.program_id(1)

    @pl.when(i == 0)
    def _():
        ubuf[0:A_HALO, :] = jnp.zeros((A_HALO, MIX_W), F32)

    @pl.when(i > 0)
    def _():
        ubuf[0:A_HALO, :] = ubuf[tt:tt + A_HALO, :]

    ubuf[A_HALO:A_HALO + tt, :] = val_ref[...] * _sigmoid(gate_ref[...])
    span = tt + A_HALO - SUBLANES
    for s in range(1, SUBLANES):
        ush[s - 1] = ubuf[s:s + span, :]
    off = A_HALO - (CONV_A_WIDTH - 1)
    for r0 in range(0, tt, A_ROWS):
        for c0 in range(0, MIX_W, LANES):
            acc = jnp.zeros((A_ROWS, LANES), F32) + cb_ref[:, c0:c0 + LANES]
            for j in range(CONV_A_WIDTH):
                s, a = (off + j) % SUBLANES, (off + j) // SUBLANES * SUBLANES
                if s == 0:
                    tap = ubuf[r0 + a:r0 + a + A_ROWS, c0:c0 + LANES]
                else:
                    tap = ush[s - 1, r0 + a:r0 + a + A_ROWS, c0:c0 + LANES]
                acc = acc + w_ref[j:j + 1, c0:c0 + LANES] * tap
            ybuf[r0:r0 + A_ROWS, c0:c0 + LANES] = acc
    y = _ln(ybuf[...], g_ref[...], b_ref[...])
    o_ref[...] = y * _sigmoid(y)


def _branch_a(z, conv_w, conv_b, ln_g, ln_b, batch, seq, tt=256):
    nt = seq // tt
    vec = lambda: pl.BlockSpec((1, MIX_W), lambda b, i: (0, 0))
    return pl.pallas_call(
        functools.partial(_conv_a_kernel, tt=tt),
        grid=(batch, nt),
        in_specs=[pl.BlockSpec((tt, MIX_W), lambda b, i: (b * nt + i, COL_A_VAL // MIX_W)),
                  pl.BlockSpec((tt, MIX_W), lambda b, i: (b * nt + i, COL_A_GATE // MIX_W)),
                  pl.BlockSpec((CONV_A_WIDTH, MIX_W), lambda b, i: (0, 0)),
                  vec(), vec(), vec()],
        out_specs=pl.BlockSpec((tt, MIX_W), lambda b, i: (b * nt + i, 0)),
        out_shape=jax.ShapeDtypeStruct((batch * seq, MIX_W), F32),
        scratch_shapes=[pltpu.VMEM((A_HALO + tt, MIX_W), F32),
                        pltpu.VMEM((SUBLANES - 1, tt + A_HALO - SUBLANES, MIX_W), F32),
                        pltpu.VMEM((tt, MIX_W), F32)],
        compiler_params=_cparams("arbitrary", "arbitrary"),
        name="branch_a_conv",
    )(z, z, conv_w, conv_b.reshape(1, -1), ln_g.reshape(1, -1), ln_b.reshape(1, -1))


def _gla_kernel(q_ref, k_ref, v_ref, r_ref, lr_ref, wa2_ref, ba_ref, g_ref, o_ref, state, *, tt):
    @pl.when(pl.program_id(1) == 0)
    def _():
        state[...] = jnp.zeros_like(state)

    c = GLA_CHUNK
    row = lax.broadcasted_iota(jnp.int32, (c, c), 0)
    col = lax.broadcasted_iota(jnp.int32, (c, c), 1)
    causal = col <= row
    tri = jnp.where(causal, 1.0, 0.0).astype(BF16)
    ones = jnp.ones((c, LANES), BF16)
    scale = GLA_DK ** -0.5
    for c0 in range(0, tt, c):
        rows = slice(c0, c0 + c)
        x = _dot(lr_ref[rows, :].astype(BF16), wa2_ref[...]) + ba_ref[...]
        la = -_softplus(-x) / GLA_TAU
        la_hi, la_lo = _split_bf16(la)
        b = _dot(tri, la_hi) + _dot(tri, la_lo)
        b_last = b[c - 1:c, :]
        q_dec = (q_ref[rows, :] * scale) * jnp.exp(b)
        k = k_ref[rows, :]
        k_inv = (k * jnp.exp(-b)).astype(BF16)
        k_end = (k * jnp.exp(b_last - b)).astype(BF16)
        q_dec = q_dec.astype(BF16)
        dsum = _dot_tn(la_hi, ones) + _dot_tn(la_lo, ones)
        for h in range(GLA_H):
            ks = slice(h * GLA_DK, (h + 1) * GLA_DK)
            vs = slice(h * GLA_DV, (h + 1) * GLA_DV)
            v = v_ref[rows, vs].astype(BF16)
            s = jnp.where(causal, _dot_nt(q_dec[:, ks], k_inv[:, ks]), 0.0)
            s_prev = state[h]
            o = _dot(s.astype(BF16), v) + _dot(q_dec[:, ks], s_prev.astype(BF16))
            state[h] = jnp.exp(dsum[ks, :]) * s_prev + _dot_tn(k_end[:, ks], v)
            o = o * lax.rsqrt(jnp.mean(o * o, axis=-1, keepdims=True) + LN_EPS) * g_ref[...]
            rg = r_ref[rows, vs]
            o_ref[rows, vs] = o * (rg * _sigmoid(rg))


def _branch_b(z, zlr, wa2_pad_bf, gla_ba, gla_norm_g, batch, seq, tt=256):
    nt = seq // tt
    hk = GLA_H * GLA_DK
    return pl.pallas_call(
        functools.partial(_gla_kernel, tt=tt),
        grid=(batch, nt),
        in_specs=[pl.BlockSpec((tt, hk), lambda b, i: (b * nt + i, COL_B_Q // hk)),
                  pl.BlockSpec((tt, hk), lambda b, i: (b * nt + i, COL_B_K // hk)),
                  pl.BlockSpec((tt, MIX_W), lambda b, i: (b * nt + i, COL_B_V // MIX_W)),
                  pl.BlockSpec((tt, MIX_W), lambda b, i: (b * nt + i, COL_B_R // MIX_W)),
                  pl.BlockSpec((tt, LANES), lambda b, i: (b * nt + i, 0)),
                  pl.BlockSpec((LANES, hk), lambda b, i: (0, 0)),
                  pl.BlockSpec((1, hk), lambda b, i: (0, 0)),
                  pl.BlockSpec((1, GLA_DV), lambda b, i: (0, 0))],
        out_specs=pl.BlockSpec((tt, MIX_W), lambda b, i: (b * nt + i, 0)),
        out_shape=jax.ShapeDtypeStruct((batch * seq, MIX_W), F32),
        scratch_shapes=[pltpu.VMEM((GLA_H, GLA_DK, GLA_DV), F32)],
        compiler_params=_cparams("arbitrary", "arbitrary"),
        name="branch_b_gla",
    )(z, z, z, z, zlr, wa2_pad_bf, gla_ba.reshape(1, -1), gla_norm_g.reshape(1, -1))


SB_PAIRS = 4
SB_QROWS = 256
SB_DEAD_LOG = -104.0


def _sb_kernel(q_ref, k_ref, v_ref, uj_ref, o_ref, kb, vb, ls_s, lk_s, w_s, carry_s, acc_s):
    qi = pl.program_id(2)
    blk = SB_BLOCK
    qr = SB_QROWS
    rows = pl.ds(pl.multiple_of(qi * qr, qr), qr)
    kb[rows, :] = k_ref[...].astype(BF16)
    vb[rows, :] = v_ref[...].astype(BF16)

    lane = lax.broadcasted_iota(jnp.int32, (qr, LANES), 1)
    row2 = lax.broadcasted_iota(jnp.int32, (2 * qr, blk), 0)
    col2 = lax.broadcasted_iota(jnp.int32, (2 * qr, blk), 1)
    ahead = jnp.bitwise_and(row2, qr - 1) - col2
    scale = SB_DH ** -0.5
    qs = []
    for p in range(SB_PAIRS):
        q = q_ref[:, p * LANES:(p + 1) * LANES] * scale
        qs.append(jnp.concatenate([jnp.where(lane < SB_DH, q, 0.0).astype(BF16),
                                   jnp.where(lane >= SB_DH, q, 0.0).astype(BF16)], axis=0))
        carry_s[p] = jnp.zeros((2 * qr, LANES), F32)
        acc_s[p] = jnp.zeros((2 * qr, LANES), F32)

    def tile_rows(j):
        return pl.ds(pl.multiple_of(j * blk, blk), blk)

    def stage_a(j, key_shift):
        for p in range(SB_PAIRS):
            ks = kb[tile_rows(j), p * LANES:(p + 1) * LANES]
            zz = _dot_nt(qs[p], ks)
            neg = -zz
            lk = jnp.minimum(neg, 0.0) - jnp.log(1.0 + jnp.exp(jnp.minimum(zz, neg)))
            ls = zz + lk
            if key_shift is not None:
                before = ahead > key_shift
                lk = jnp.where(before, lk, 0.0)
                ls = jnp.where(before, ls, -1e30)
            hi, lo = _split_bf16(lk)
            ls_s[p] = ls
            lk_s[p] = jnp.concatenate([hi, lo], axis=1)

    def stage_b():
        for p in range(SB_PAIRS):
            r = _dot(lk_s[p], uj_ref[...])
            carry = carry_s[p]
            w_s[p] = jnp.exp(ls_s[p] + carry + r[:, :LANES]).astype(BF16)
            carry_s[p] = carry + r[:, LANES:]

    def stage_c(j):
        for p in range(SB_PAIRS):
            acc_s[p] = acc_s[p] + _dot(w_s[p], vb[tile_rows(j), p * LANES:(p + 1) * LANES])

    tiles_per_q = qr // blk
    last = tiles_per_q * qi + tiles_per_q - 1
    stage_a(last, (tiles_per_q - 1) * blk)
    stage_b()
    stage_a(last - 1, (tiles_per_q - 2) * blk)

    def body(state):
        m, _ = state
        for mm in (m, m + 1):
            stage_c(last - mm + 2)
            stage_b()
            stage_a(last - mm, None)
        top = carry_s[0]
        for p in range(1, SB_PAIRS):
            top = jnp.maximum(top, carry_s[p])
        dead = (jnp.max(top) < SB_DEAD_LOG).astype(jnp.int32)
        return m + 2, dead

    m, dead = lax.while_loop(lambda st: jnp.logical_and(st[0] <= last, st[1] == 0), body,
                             (jnp.int32(2), jnp.int32(0)))
    stage_c(last - m + 2)

    @pl.when(dead == 0)
    def _():
        stage_b()
        stage_c(0)

    for p in range(SB_PAIRS):
        acc = acc_s[p]
        o_ref[:, p * LANES:(p + 1) * LANES] = jnp.where(lane < SB_DH, acc[:qr], acc[qr:])


def _branch_c(z, batch, seq):
    assert SB_QROWS == 2 * SB_BLOCK
    nq = seq // SB_QROWS
    width = SB_PAIRS * LANES
    groups = SB_H * SB_DH // width
    blk = SB_BLOCK
    row = lax.broadcasted_iota(jnp.int32, (blk, blk), 0)
    col = lax.broadcasted_iota(jnp.int32, (blk, blk), 1)
    u = jnp.where(row > col, 1.0, 0.0)
    uj = jnp.concatenate([u, jnp.ones((blk, blk), F32)], axis=1)
    uj = jnp.concatenate([uj, uj], axis=0).astype(BF16)
    blockspec = lambda col0: pl.BlockSpec((SB_QROWS, width), lambda b, g, i: (b * nq + i, col0 // width + g))
    stage = lambda lanes, dt: pltpu.VMEM((SB_PAIRS, 2 * SB_QROWS, lanes), dt)
    return pl.pallas_call(
        _sb_kernel,
        grid=(batch, groups, nq),
        in_specs=[blockspec(COL_C_Q), blockspec(COL_C_K), blockspec(COL_C_V),
                  pl.BlockSpec((2 * blk, 2 * blk), lambda b, g, i: (0, 0))],
        out_specs=pl.BlockSpec((SB_QROWS, width), lambda b, g, i: (b * nq + i, g)),
        out_shape=jax.ShapeDtypeStruct((batch * seq, MIX_W), F32),
        scratch_shapes=[pltpu.VMEM((seq, width), BF16), pltpu.VMEM((seq, width), BF16),
                        stage(LANES, F32), stage(2 * LANES, BF16), stage(LANES, BF16),
                        stage(LANES, F32), stage(LANES, F32)],
        compiler_params=_cparams("arbitrary", "arbitrary", "arbitrary"),
        name="branch_c_stick_breaking",
    )(z, z, z, uj)


D_HALO = 8


def _lru_kernel(x_ref, gate_ref, cw_ref, cb_ref, wa_ref, ba_ref, wx_ref, bx_ref, lam_ref, o_ref,
                xbuf, abuf, ubuf, hbuf, hprev, *, tt):
    i = pl.program_id(1)

    @pl.when(i == 0)
    def _():
        xbuf[0:D_HALO, :] = jnp.zeros((D_HALO, MIX_W), F32)
        hprev[...] = jnp.zeros_like(hprev)

    @pl.when(i > 0)
    def _():
        xbuf[0:D_HALO, :] = xbuf[tt:tt + D_HALO, :]

    xbuf[D_HALO:D_HALO + tt, :] = x_ref[...]
    off = D_HALO - (LRU_CONV - 1)
    xc = jnp.zeros((tt, MIX_W), F32) + cb_ref[...]
    for j in range(LRU_CONV):
        xc = xc + cw_ref[j:j + 1, :] * xbuf[off + j:off + j + tt, :]
    xc_bf = xc.astype(BF16)
    r = _sigmoid(_dot(xc_bf, wa_ref[...]) + ba_ref[...])
    gi = _sigmoid(_dot(xc_bf, wx_ref[...]) + bx_ref[...])
    log_a = LRU_C * r * (-_softplus(-lam_ref[...]))
    a = jnp.exp(log_a)
    abuf[...] = a
    ubuf[...] = jnp.sqrt(-jnp.tanh(log_a) * (a * a + 1.0)) * (gi * xc)

    def step(t, h):
        h = abuf[pl.ds(t, 1), :] * h + ubuf[pl.ds(t, 1), :]
        hbuf[pl.ds(t, 1), :] = h
        return h

    hprev[...] = lax.fori_loop(0, tt, step, hprev[...], unroll=8)
    g = gate_ref[...]
    gelu = 0.5 * g * (1.0 + jnp.tanh(0.7978845608028654 * (g + 0.044715 * g * g * g)))
    o_ref[...] = hbuf[...] * gelu


def _branch_d(z, conv_w, conv_b, wa_bd_bf, ba, wx_bd_bf, bx, lam, batch, seq, tt=256):
    nt = seq // tt
    vec = lambda: pl.BlockSpec((1, MIX_W), lambda b, i: (0, 0))
    mat = lambda: pl.BlockSpec((MIX_W, MIX_W), lambda b, i: (0, 0))
    return pl.pallas_call(
        functools.partial(_lru_kernel, tt=tt),
        grid=(batch, nt),
        in_specs=[pl.BlockSpec((tt, MIX_W), lambda b, i: (b * nt + i, COL_D_X // MIX_W)),
                  pl.BlockSpec((tt, MIX_W), lambda b, i: (b * nt + i, COL_D_G // MIX_W)),
                  pl.BlockSpec((LRU_CONV, MIX_W), lambda b, i: (0, 0)),
                  vec(), mat(), vec(), mat(), vec(), vec()],
        out_specs=pl.BlockSpec((tt, MIX_W), lambda b, i: (b * nt + i, 0)),
        out_shape=jax.ShapeDtypeStruct((batch * seq, MIX_W), F32),
        scratch_shapes=[pltpu.VMEM((D_HALO + tt, MIX_W), F32), pltpu.VMEM((tt, MIX_W), F32),
                        pltpu.VMEM((tt, MIX_W), F32), pltpu.VMEM((tt, MIX_W), F32),
                        pltpu.VMEM((1, MIX_W), F32)],
        compiler_params=_cparams("arbitrary", "arbitrary"),
        name="branch_d_rglru",
    )(z, z, conv_w, conv_b.reshape(1, -1), wa_bd_bf, ba.reshape(1, -1), wx_bd_bf, bx.reshape(1, -1),
      lam.reshape(1, -1))


def _merge_kernel(ya_ref, yb_ref, yc_ref, yd_ref, g0_ref, g1_ref, g2_ref, g3_ref, h_ref, wb_ref, wo_ref,
                  bo_ref, lg_ref, lb_ref, o_ref, *, alpha):
    merged = None
    for n, (y_ref, g_ref) in enumerate(((ya_ref, g0_ref), (yb_ref, g1_ref), (yc_ref, g2_ref), (yd_ref, g3_ref))):
        term = _sigmoid(g_ref[...]) * _dot(y_ref[...].astype(BF16), wb_ref[n])
        merged = term if merged is None else merged + term
    mix = _dot(merged.astype(BF16), wo_ref[...]) + bo_ref[...]
    o_ref[...] = _ln(alpha * h_ref[...] + mix, lg_ref[...], lb_ref[...])


def _merge(ya, yb, yc, yd, z, h, w_branch_bf, w_out_bf, b_out, ln_g, ln_b, alpha, tm=256):
    t, d = h.shape
    ysp = lambda: pl.BlockSpec((tm, MIX_W), lambda i: (i, 0))
    gsp = lambda n: pl.BlockSpec((tm, d), lambda i: (i, COL_G // d + n))
    vec = lambda: pl.BlockSpec((1, d), lambda i: (0, 0))
    return pl.pallas_call(
        functools.partial(_merge_kernel, alpha=alpha),
        grid=(t // tm,),
        in_specs=[ysp(), ysp(), ysp(), ysp(), gsp(0), gsp(1), gsp(2), gsp(3),
                  pl.BlockSpec((tm, d), lambda i: (i, 0)),
                  pl.BlockSpec((4, MIX_W, d), lambda i: (0, 0, 0)),
                  pl.BlockSpec((d, d), lambda i: (0, 0)),
                  vec(), vec(), vec()],
        out_specs=pl.BlockSpec((tm, d), lambda i: (i, 0)),
        out_shape=jax.ShapeDtypeStruct((t, d), F32),
        compiler_params=_cparams("arbitrary"),
        name="merge_out_ln",
    )(ya, yb, yc, yd, z, z, z, z, h, w_branch_bf, w_out_bf, b_out.reshape(1, d), ln_g.reshape(1, d),
      ln_b.reshape(1, d))


def _xattn_kernel(h_ref, wq_ref, kv_ref, wo_ref, lg_ref, lb_ref, o_ref, obuf, *, alpha, d):
    h = h_ref[...]
    q = _dot(h.astype(BF16), wq_ref[...])
    dh = d // MEM_H
    scale = dh ** -0.5
    for hd in range(MEM_H):
        cs = slice(hd * dh, (hd + 1) * dh)
        k = kv_ref[:, cs].astype(BF16)
        v = kv_ref[:, d + hd * dh:d + (hd + 1) * dh].astype(BF16)
        s = _dot_nt(q[:, cs].astype(BF16), k) * scale
        s = s - jnp.max(s, axis=-1, keepdims=True)
        p = jnp.exp(s)
        p = p / jnp.sum(p, axis=-1, keepdims=True)
        obuf[:, cs] = _dot(p.astype(BF16), v)
    ca = _dot(obuf[...].astype(BF16), wo_ref[...])
    o_ref[...] = _ln(alpha * h + ca, lg_ref[...], lb_ref[...])


def _xattn(h, kv, wq_bf, wo_bf, ln_g, ln_b, alpha, seq, n_mem, tm=256):
    t, d = h.shape
    per_batch = seq // tm
    vec = lambda: pl.BlockSpec((1, d), lambda i: (0, 0))
    return pl.pallas_call(
        functools.partial(_xattn_kernel, alpha=alpha, d=d),
        grid=(t // tm,),
        in_specs=[pl.BlockSpec((tm, d), lambda i: (i, 0)),
                  pl.BlockSpec((d, d), lambda i: (0, 0)),
                  pl.BlockSpec((n_mem, 2 * d), lambda i: (i // per_batch, 0)),
                  pl.BlockSpec((d, d), lambda i: (0, 0)),
                  vec(), vec()],
        out_specs=pl.BlockSpec((tm, d), lambda i: (i, 0)),
        out_shape=jax.ShapeDtypeStruct((t, d), F32),
        scratch_shapes=[pltpu.VMEM((tm, d), F32)],
        compiler_params=_cparams("arbitrary"),
        name="xattn_ln",
    )(h, wq_bf, kv, wo_bf, ln_g.reshape(1, d), ln_b.reshape(1, d))


ROUTE_IDX, ROUTE_GATE, ROUTE_RANK = 0, 4, 8


def _router_kernel(h_ref, whi_ref, wlo_ref, b_ref, lt_ref, route_ref, cnt_ref, cnt):
    @pl.when(pl.program_id(0) == 0)
    def _():
        cnt[...] = jnp.zeros_like(cnt)

    x_hi, x_lo = _split_bf16(h_ref[...])
    logits = _dot(x_hi, whi_ref[...]) + _dot(x_lo, whi_ref[...]) + _dot(x_hi, wlo_ref[...]) + b_ref[...]
    tm = logits.shape[0]
    lane = lax.broadcasted_iota(jnp.int32, (tm, LANES), 1).astype(F32)
    cur = logits
    vals, idxs = [], []
    for _ in range(TOP_K):
        m = jnp.max(cur, axis=-1, keepdims=True)
        idx = jnp.min(jnp.where(cur == m, lane, float(LANES)), axis=-1, keepdims=True)
        vals.append(m)
        idxs.append(idx)
        cur = jnp.where(lane == idx, -jnp.inf, cur)
    ex = [jnp.exp(v - vals[0]) for v in vals]
    denom = ex[0] + ex[1] + ex[2] + ex[3]
    onehot = jnp.zeros((tm, LANES), F32)
    for idx in idxs:
        onehot = onehot + jnp.where(lane == idx, 1.0, 0.0)
    before = _dot(lt_ref[...], onehot.astype(BF16)) + cnt[...]
    route = jnp.zeros((tm, LANES), F32)
    for k in range(TOP_K):
        rank = jnp.sum(jnp.where(lane == idxs[k], before, 0.0), axis=-1, keepdims=True)
        route = jnp.where(lane == float(ROUTE_IDX + k), idxs[k], route)
        route = jnp.where(lane == float(ROUTE_GATE + k), ex[k] / denom, route)
        route = jnp.where(lane == float(ROUTE_RANK + k), rank, route)
    route_ref[...] = route
    cnt[...] = cnt[...] + jnp.sum(onehot, axis=0, keepdims=True)
    cnt_ref[...] = cnt[...]


def _router(h, router_w, router_b, tm=256):
    t, d = h.shape
    w_pad = jnp.zeros((d, LANES), F32).at[:, :N_EXPERTS].set(router_w)
    w_hi = w_pad.astype(BF16)
    w_lo = (w_pad - w_hi.astype(F32)).astype(BF16)
    b_pad = jnp.full((1, LANES), -1e30, F32).at[0, :N_EXPERTS].set(router_b)
    row = lax.broadcasted_iota(jnp.int32, (tm, tm), 0)
    col = lax.broadcasted_iota(jnp.int32, (tm, tm), 1)
    lower = jnp.where(col < row, 1.0, 0.0).astype(BF16)
    return pl.pallas_call(
        _router_kernel,
        grid=(t // tm,),
        in_specs=[pl.BlockSpec((tm, d), lambda i: (i, 0)),
                  pl.BlockSpec((d, LANES), lambda i: (0, 0)),
                  pl.BlockSpec((d, LANES), lambda i: (0, 0)),
                  pl.BlockSpec((1, LANES), lambda i: (0, 0)),
                  pl.BlockSpec((tm, tm), lambda i: (0, 0))],
        out_specs=[pl.BlockSpec((tm, LANES), lambda i: (i, 0)),
                   pl.BlockSpec((1, LANES), lambda i: (0, 0))],
        out_shape=[jax.ShapeDtypeStruct((t, LANES), F32), jax.ShapeDtypeStruct((1, LANES), F32)],
        scratch_shapes=[pltpu.VMEM((1, LANES), F32)],
        compiler_params=_cparams("arbitrary"),
        name="moe_router",
    )(h, w_hi, w_lo, b_pad, lower)


EXPERT_BUFS = 3
ROW_TILES = 8


def _expert_kernel(be_ref, nu_ref, tok_ref, x_hbm, w1_ref, b1_ref, w2_ref, b2_ref, o_ref, xbuf, w1b, w2b, sems, *, ff):
    i = pl.program_id(0)
    n_used = nu_ref[0]

    def row_copy(blk, slot, r):
        return pltpu.make_async_copy(x_hbm.at[pl.ds(tok_ref[blk * MOE_BLOCK + r], 1)],
                                     xbuf.at[slot, pl.ds(r, 1)], sems.at[slot])

    def issue(blk, slot):
        def body(r, _):
            row_copy(blk, slot, r).start()
            return 0
        lax.fori_loop(0, MOE_BLOCK, body, 0, unroll=8)

    def drain(slot):
        pltpu.make_async_copy(x_hbm.at[pl.ds(0, MOE_BLOCK)], xbuf.at[slot], sems.at[slot]).wait()

    def ffn(slot, prefetch):
        x = xbuf[slot].astype(BF16)
        if prefetch:
            ahead = lax.rem(i + 2, EXPERT_BUFS)
            for r in range(MOE_BLOCK):
                row_copy(i + 2, ahead, r).start()
        hcat = _dot(x, w1b[...]) + b1_ref[0]
        g = jnp.minimum(hcat[:, :ff], SWIGLU_LIMIT)
        lin = jnp.clip(hcat[:, ff:], -SWIGLU_LIMIT, SWIGLU_LIMIT)
        act = g * _sigmoid(SWIGLU_ALPHA * g) * (lin + 1.0)
        out = _dot(act.astype(BF16), w2b[...]) + b2_ref[0]
        for j in range(ROW_TILES):
            o_ref[pl.ds(j, MOE_BLOCK, stride=ROW_TILES), :] = out[:, j * LANES:(j + 1) * LANES]

    @pl.when(i < n_used)
    def _():
        slot = lax.rem(i, EXPERT_BUFS)

        @pl.when(i == 0)
        def _():
            issue(0, 0)

            @pl.when(n_used > 1)
            def _():
                issue(1, 1)

        @pl.when(jnp.logical_or(i == 0, be_ref[i] != be_ref[jnp.maximum(i - 1, 0)]))
        def _():
            w1b[...] = w1_ref[0, 0].astype(BF16)
            w2b[...] = w2_ref[0, 0].astype(BF16)

        drain(slot)

        @pl.when(i + 2 < n_used)
        def _():
            ffn(slot, True)

        @pl.when(i + 2 >= n_used)
        def _():
            ffn(slot, False)

    @pl.when(i >= n_used)
    def _():
        o_ref[...] = jnp.zeros_like(o_ref)


def _experts(h, slot_tok, block_e, n_used, w1, b1, w2, b2, layer):
    t, d = h.shape
    ff = w2.shape[2]
    n_slots = slot_tok.shape[0]
    n_blocks = n_slots // MOE_BLOCK
    return pl.pallas_call(
        functools.partial(_expert_kernel, ff=ff),
        grid_spec=pltpu.PrefetchScalarGridSpec(
            num_scalar_prefetch=3,
            grid=(n_blocks,),
            in_specs=[pl.BlockSpec(memory_space=pl.ANY),
                      pl.BlockSpec((1, 1, d, 2 * ff), lambda i, be, nu, tok: (layer, be[i], 0, 0)),
                      pl.BlockSpec((1, 1, 2 * ff), lambda i, be, nu, tok: (be[i], 0, 0)),
                      pl.BlockSpec((1, 1, ff, d), lambda i, be, nu, tok: (layer, be[i], 0, 0)),
                      pl.BlockSpec((1, 1, d), lambda i, be, nu, tok: (be[i], 0, 0))],
            out_specs=pl.BlockSpec((MOE_BLOCK * ROW_TILES, LANES), lambda i, be, nu, tok: (i, 0)),
            scratch_shapes=[pltpu.VMEM((EXPERT_BUFS, MOE_BLOCK, d), F32), pltpu.VMEM((d, 2 * ff), BF16),
                            pltpu.VMEM((ff, d), BF16), pltpu.SemaphoreType.DMA((EXPERT_BUFS,))],
        ),
        out_shape=jax.ShapeDtypeStruct((n_slots * ROW_TILES, LANES), F32),
        compiler_params=_cparams("arbitrary"),
        name="moe_experts",
    )(block_e, n_used, slot_tok, h, w1, b1.reshape(N_EXPERTS, 1, 2 * ff), w2, b2.reshape(N_EXPERTS, 1, d))


COMBINE_TOKENS = 128


def _combine_kernel(dest_ref, ys_hbm, route_ref, h_ref, lg_ref, lb_ref, o_ref, buf, sems, *, alpha):
    i = pl.program_id(0)
    n = pl.num_programs(0)
    tm = COMBINE_TOKENS

    def row_copy(step, slot, t, k):
        src = pl.multiple_of(dest_ref[(step * tm + t) * TOP_K + k] * ROW_TILES, ROW_TILES)
        return pltpu.make_async_copy(ys_hbm.at[pl.ds(src, ROW_TILES)], buf.at[slot, k, pl.ds(t * ROW_TILES, ROW_TILES)],
                                     sems.at[slot])

    slot = lax.rem(i, 2)

    @pl.when(i == 0)
    def _():
        def body(t, _):
            for k in range(TOP_K):
                row_copy(0, 0, t, k).start()
            return 0
        lax.fori_loop(0, tm, body, 0)

    @pl.when(i + 1 < n)
    def _():
        for t in range(tm):
            for k in range(TOP_K):
                row_copy(i + 1, 1 - slot, t, k).start()

    for k in range(TOP_K):
        pltpu.make_async_copy(ys_hbm.at[pl.ds(0, tm * ROW_TILES)], buf.at[slot, k], sems.at[slot]).wait()
    route = route_ref[...]
    pieces = []
    for j in range(ROW_TILES):
        part = jnp.zeros((tm, LANES), F32)
        for k in range(TOP_K):
            part = part + route[:, ROUTE_GATE + k:ROUTE_GATE + k + 1] * buf[slot, k, pl.ds(j, tm, stride=ROW_TILES), :]
        pieces.append(part)
    ff = jnp.concatenate(pieces, axis=1)
    o_ref[...] = _ln(alpha * h_ref[...] + ff, lg_ref[...], lb_ref[...])


def _combine(ys, dest_flat, route, h, ln_g, ln_b, alpha):
    t, d = h.shape
    tm = COMBINE_TOKENS
    vec = lambda: pl.BlockSpec((1, d), lambda i, dst: (0, 0))
    return pl.pallas_call(
        functools.partial(_combine_kernel, alpha=alpha),
        grid_spec=pltpu.PrefetchScalarGridSpec(
            num_scalar_prefetch=1,
            grid=(t // tm,),
            in_specs=[pl.BlockSpec(memory_space=pl.ANY),
                      pl.BlockSpec((tm, LANES), lambda i, dst: (i, 0)),
                      pl.BlockSpec((tm, d), lambda i, dst: (i, 0)),
                      vec(), vec()],
            out_specs=pl.BlockSpec((tm, d), lambda i, dst: (i, 0)),
            scratch_shapes=[pltpu.VMEM((2, TOP_K, tm * ROW_TILES, LANES), F32), pltpu.SemaphoreType.DMA((2,))],
        ),
        out_shape=jax.ShapeDtypeStruct((t, d), F32),
        compiler_params=_cparams("arbitrary"),
        name="moe_combine_ln",
    )(dest_flat, ys, route, h, ln_g.reshape(1, d), ln_b.reshape(1, d))


def _moe(h, router_w, router_b, w1, b1, w2, b2, ln_g, ln_b, alpha, layer):
    t, d = h.shape
    route, cnt = _router(h, router_w, router_b)
    idx = route[:, ROUTE_IDX:ROUTE_IDX + TOP_K].astype(jnp.int32)
    rank = route[:, ROUTE_RANK:ROUTE_RANK + TOP_K].astype(jnp.int32)
    counts = cnt[0, :N_EXPERTS].astype(jnp.int32)
    padded = ((counts + MOE_BLOCK - 1) // MOE_BLOCK) * MOE_BLOCK
    pend = jnp.cumsum(padded)
    pstart = pend - padded
    dest = (pstart[idx] + rank).reshape(t * TOP_K)
    n_blocks = t * TOP_K // MOE_BLOCK + N_EXPERTS
    n_used = (pend[-1:] // MOE_BLOCK).astype(jnp.int32)
    block_start = jnp.arange(n_blocks, dtype=jnp.int32) * MOE_BLOCK
    block_e = jnp.minimum(jnp.sum((pend[None, :] <= block_start[:, None]).astype(jnp.int32), axis=1), N_EXPERTS - 1)
    tok = jnp.arange(t * TOP_K, dtype=jnp.int32) // TOP_K
    slot_tok = jnp.zeros((n_blocks * MOE_BLOCK,), jnp.int32).at[dest].set(tok, unique_indices=True)
    ys = _experts(h, slot_tok, block_e, n_used, w1, b1, w2, b2, layer)
    return _combine(ys, dest, route, h, ln_g, ln_b, alpha)


def _block_diag(w):
    n, bw, _ = w.shape
    eye = jnp.eye(n, dtype=w.dtype)
    return (eye[:, None, :, None] * w[:, :, None, :]).reshape(n * bw, n * bw)


def kernel(x, mem, ln0_g, ln0_b, w_in, b_in, conv_a_w, conv_a_b, ln_a_g, ln_a_b, gla_wa2, gla_ba, gla_norm_g, conv_d_w, conv_d_b, lru_wa, lru_ba, lru_wx, lru_bx, lru_lambda, w_branch, w_out, b_out, ln1_g, ln1_b, ca_wq, ca_wk, ca_wv, ca_wo, ln2_g, ln2_b, router_w, router_b, moe_w1, moe_b1, moe_w2, moe_b2, ln3_g, ln3_b):
    batch, seq, d = x.shape
    n_mem = mem.shape[1]
    depth = w_in.shape[0]
    alpha = (2 * depth) ** 0.25
    t = batch * seq
    h = _layer_norm(x.reshape(t, d), ln0_g, ln0_b)
    mem2 = mem.reshape(batch * n_mem, d)
    w_main_all = _repack_w_in(w_in)
    w_lr_all = jnp.zeros((depth, d, LANES), F32).at[:, :, :GLA_LOWRANK].set(w_in[:, :, LR_START:LR_END]).astype(BF16)
    w_kv_all = jnp.concatenate([ca_wk, ca_wv], axis=2).astype(BF16)
    for l in range(depth):
        b_main = jnp.concatenate([b_in[l][:LR_START], b_in[l][LR_END:]])
        b_lr = jnp.zeros((LANES,), F32).at[:GLA_LOWRANK].set(b_in[l][LR_START:LR_END])
        z = _matmul_bias(h, w_main_all, b_main, tm=1024, tn=1024, layer=l)
        zlr = _matmul_bias(h, w_lr_all, b_lr, tm=1024, tn=LANES, layer=l)
        wa2_pad = jnp.zeros((LANES, GLA_H * GLA_DK), F32).at[:GLA_LOWRANK].set(gla_wa2[l]).astype(BF16)
        ya = _branch_a(z, conv_a_w[l], conv_a_b[l], ln_a_g[l], ln_a_b[l], batch, seq)
        yb = _branch_b(z, zlr, wa2_pad, gla_ba[l], gla_norm_g[l], batch, seq)
        yc = _branch_c(z, batch, seq)
        yd = _branch_d(z, conv_d_w[l], conv_d_b[l], _block_diag(lru_wa[l]).astype(BF16), lru_ba[l],
                       _block_diag(lru_wx[l]).astype(BF16), lru_bx[l], lru_lambda[l], batch, seq)
        h = _merge(ya, yb, yc, yd, z, h, w_branch[l].astype(BF16), w_out[l].astype(BF16), b_out[l],
                   ln1_g[l], ln1_b[l], alpha)
        kv = _matmul_bias(mem2, w_kv_all, jnp.zeros((2 * d,), F32), tm=batch * n_mem, tn=512, layer=l)
        h = _xattn(h, kv, ca_wq[l].astype(BF16), ca_wo[l].astype(BF16), ln2_g[l], ln2_b[l], alpha, seq, n_mem)
        h = _moe(h, router_w[l], router_b[l], moe_w1, moe_b1[l], moe_w2, moe_b2[l], ln3_g[l], ln3_b[l], alpha, l)
    return h.reshape(batch, seq, d)
```

```python
import functools

import jax
import jax.numpy as jnp
from jax import lax
from jax.experimental import pallas as pl
from jax.experimental.pallas import tpu as pltpu

F32 = jnp.float32
BF16 = jnp.bfloat16

MIX_W = 512
CONV_A_WIDTH = 31
GLA_H, GLA_DK, GLA_DV = 4, 64, 128
GLA_LOWRANK = 16
GLA_TAU = 16.0
GLA_CHUNK = 64
SB_H, SB_DH, SB_BLOCK = 8, 64, 128
LRU_C = 8.0
LRU_CONV = 4
MEM_H = 4
N_EXPERTS, TOP_K = 32, 4
MOE_BLOCK = 256
SWIGLU_LIMIT = 7.0
SWIGLU_ALPHA = 1.702
LN_EPS = 1e-5
LANES = 128
SUBLANES = 8
VMEM_LIMIT = 48 * 1024 * 1024

COL_A_VAL, COL_A_GATE = 0, 512
COL_B_Q, COL_B_K, COL_B_V, COL_B_R = 1024, 1280, 1536, 2048
COL_C_Q, COL_C_K, COL_C_V = 2560, 3072, 3584
COL_D_X, COL_D_G = 4096, 4608
COL_G = 5120
Z_COLS = 9216
LR_START, LR_END = 2560, 2576


def _cparams(*sem):
    return pltpu.CompilerParams(dimension_semantics=sem, vmem_limit_bytes=VMEM_LIMIT)


def _ln(x, g, b):
    mu = jnp.mean(x, axis=-1, keepdims=True)
    xc = x - mu
    var = jnp.mean(xc * xc, axis=-1, keepdims=True)
    return xc * lax.rsqrt(var + LN_EPS) * g + b


def _sigmoid(x):
    return 1.0 / (1.0 + jnp.exp(-x))


def _softplus(x):
    return jnp.maximum(x, 0.0) + jnp.log1p(jnp.exp(-jnp.abs(x)))


def _split_bf16(x):
    hi = x.astype(BF16)
    lo = (x - hi.astype(F32)).astype(BF16)
    return hi, lo


def _dot(a, b):
    return jnp.dot(a, b, preferred_element_type=F32)


def _dot_nt(a, b):
    return lax.dot_general(a, b, (((1,), (1,)), ((), ())), preferred_element_type=F32)


def _dot_tn(a, b):
    return lax.dot_general(a, b, (((0,), (0,)), ((), ())), preferred_element_type=F32)


def _ln_kernel(x_ref, g_ref, b_ref, o_ref):
    o_ref[...] = _ln(x_ref[...], g_ref[...], b_ref[...])


def _layer_norm(x, g, b, tm=512):
    t, d = x.shape
    return pl.pallas_call(
        _ln_kernel,
        grid=(t // tm,),
        in_specs=[pl.BlockSpec((tm, d), lambda i: (i, 0)),
                  pl.BlockSpec((1, d), lambda i: (0, 0)),
                  pl.BlockSpec((1, d), lambda i: (0, 0))],
        out_specs=pl.BlockSpec((tm, d), lambda i: (i, 0)),
        out_shape=jax.ShapeDtypeStruct((t, d), F32),
        compiler_params=_cparams("arbitrary"),
        name="layer_norm",
    )(x, g.reshape(1, d), b.reshape(1, d))


def _mm_kernel(a_ref, w_ref, b_ref, o_ref, abf_ref):
    @pl.when(pl.program_id(1) == 0)
    def _():
        abf_ref[...] = a_ref[...].astype(BF16)

    o_ref[...] = _dot(abf_ref[...], w_ref[0]) + b_ref[...]


def _matmul_bias(a, w_bf, b, tm, tn, layer=0):
    m, k = a.shape
    n = w_bf.shape[2]
    return pl.pallas_call(
        _mm_kernel,
        grid=(m // tm, n // tn),
        in_specs=[pl.BlockSpec((tm, k), lambda i, j: (i, 0)),
                  pl.BlockSpec((1, k, tn), lambda i, j: (layer, 0, j)),
                  pl.BlockSpec((1, tn), lambda i, j: (0, j))],
        out_specs=pl.BlockSpec((tm, tn), lambda i, j: (i, j)),
        out_shape=jax.ShapeDtypeStruct((m, n), F32),
        scratch_shapes=[pltpu.VMEM((tm, k), BF16)],
        compiler_params=_cparams("arbitrary", "arbitrary"),
        name="matmul_bias",
    )(a, w_bf, b.reshape(1, n))


PACK_TILE = 512


def _repack_kernel(a_ref, b_ref, o_ref):
    j = pl.program_id(1)
    lr = LR_END - LR_START

    @pl.when(j < LR_START // PACK_TILE)
    def _():
        o_ref[0] = a_ref[0].astype(BF16)

    @pl.when(j >= LR_START // PACK_TILE)
    def _():
        x = jnp.concatenate([a_ref[0], b_ref[0]], axis=1)
        o_ref[0] = pltpu.roll(x, x.shape[1] - lr, axis=1)[:, :PACK_TILE].astype(BF16)


def _repack_w_in(w_in):
    depth, d, _ = w_in.shape
    assert LR_START % PACK_TILE == 0 and Z_COLS % PACK_TILE == 0
    return pl.pallas_call(
        _repack_kernel,
        grid=(depth, Z_COLS // PACK_TILE),
        in_specs=[pl.BlockSpec((1, d, PACK_TILE), lambda l, j: (l, 0, j)),
                  pl.BlockSpec((1, d, LANES), lambda l, j: (l, 0, (j + 1) * (PACK_TILE // LANES)))],
        out_specs=pl.BlockSpec((1, d, PACK_TILE), lambda l, j: (l, 0, j)),
        out_shape=jax.ShapeDtypeStruct((depth, d, Z_COLS), BF16),
        compiler_params=_cparams("arbitrary", "arbitrary"),
        name="repack_w_in",
    )(w_in, w_in)


A_HALO = 32
A_ROWS = 64


def _conv_a_kernel(val_ref, gate_ref, w_ref, cb_ref, g_ref, b_ref, o_ref, ubuf, ush, ybuf, *, tt):
    i = pl.program_id(1)

    @pl.when(i == 0)
    def _():
        ubuf[0:A_HALO, :] = jnp.zeros((A_HALO, MIX_W), F32)

    @pl.when(i > 0)
    def _():
        ubuf[0:A_HALO, :] = ubuf[tt:tt + A_HALO, :]

    ubuf[A_HALO:A_HALO + tt, :] = val_ref[...] * _sigmoid(gate_ref[...])
    span = tt + A_HALO - SUBLANES
    for s in range(1, SUBLANES):
        ush[s - 1] = ubuf[s:s + span, :]
    off = A_HALO - (CONV_A_WIDTH - 1)
    for r0 in range(0, tt, A_ROWS):
        for c0 in range(0, MIX_W, LANES):
            acc = jnp.zeros((A_ROWS, LANES), F32) + cb_ref[:, c0:c0 + LANES]
            for j in range(CONV_A_WIDTH):
                s, a = (off + j) % SUBLANES, (off + j) // SUBLANES * SUBLANES
                if s == 0:
                    tap = ubuf[r0 + a:r0 + a + A_ROWS, c0:c0 + LANES]
                else:
                    tap = ush[s - 1, r0 + a:r0 + a + A_ROWS, c0:c0 + LANES]
                acc = acc + w_ref[j:j + 1, c0:c0 + LANES] * tap
            ybuf[r0:r0 + A_ROWS, c0:c0 + LANES] = acc
    y = _ln(ybuf[...], g_ref[...], b_ref[...])
    o_ref[...] = y * _sigmoid(y)


def _branch_a(z, conv_w, conv_b, ln_g, ln_b, batch, seq, tt=256):
    nt = seq // tt
    vec = lambda: pl.BlockSpec((1, MIX_W), lambda b, i: (0, 0))
    return pl.pallas_call(
        functools.partial(_conv_a_kernel, tt=tt),
        grid=(batch, nt),
        in_specs=[pl.BlockSpec((tt, MIX_W), lambda b, i: (b * nt + i, COL_A_VAL // MIX_W)),
                  pl.BlockSpec((tt, MIX_W), lambda b, i: (b * nt + i, COL_A_GATE // MIX_W)),
                  pl.BlockSpec((CONV_A_WIDTH, MIX_W), lambda b, i: (0, 0)),
                  vec(), vec(), vec()],
        out_specs=pl.BlockSpec((tt, MIX_W), lambda b, i: (b * nt + i, 0)),
        out_shape=jax.ShapeDtypeStruct((batch * seq, MIX_W), F32),
        scratch_shapes=[pltpu.VMEM((A_HALO + tt, MIX_W), F32),
                        pltpu.VMEM((SUBLANES - 1, tt + A_HALO - SUBLANES, MIX_W), F32),
                        pltpu.VMEM((tt, MIX_W), F32)],
        compiler_params=_cparams("arbitrary", "arbitrary"),
        name="branch_a_conv",
    )(z, z, conv_w, conv_b.reshape(1, -1), ln_g.reshape(1, -1), ln_b.reshape(1, -1))


def _gla_kernel(q_ref, k_ref, v_ref, r_ref, lr_ref, wa2_ref, ba_ref, g_ref, o_ref, state, *, tt):
    @pl.when(pl.program_id(1) == 0)
    def _():
        state[...] = jnp.zeros_like(state)

    c = GLA_CHUNK
    row = lax.broadcasted_iota(jnp.int32, (c, c), 0)
    col = lax.broadcasted_iota(jnp.int32, (c, c), 1)
    causal = col <= row
    tri = jnp.where(causal, 1.0, 0.0).astype(BF16)
    ones = jnp.ones((c, LANES), BF16)
    scale = GLA_DK ** -0.5
    chunks = range(0, tt, c)
    q_decs, kvs, decays, o_intras = [], [], [], []
    for c0 in chunks:
        rows = slice(c0, c0 + c)
        x = _dot(lr_ref[rows, :].astype(BF16), wa2_ref[...]) + ba_ref[...]
        la = -_softplus(-x) / GLA_TAU
        la_hi, la_lo = _split_bf16(la)
        b = _dot(tri, la_hi) + _dot(tri, la_lo)
        b_last = b[c - 1:c, :]
        q_dec = ((q_ref[rows, :] * scale) * jnp.exp(b)).astype(BF16)
        k = k_ref[rows, :]
        k_inv = (k * jnp.exp(-b)).astype(BF16)
        k_end = (k * jnp.exp(b_last - b)).astype(BF16)
        dsum = _dot_tn(la_hi, ones) + _dot_tn(la_lo, ones)
        q_decs.append(q_dec)
        decays.append(jnp.exp(dsum))
        kv_c, oi_c = [], []
        for h in range(GLA_H):
            ks = slice(h * GLA_DK, (h + 1) * GLA_DK)
            v = v_ref[rows, h * GLA_DV:(h + 1) * GLA_DV].astype(BF16)
            s = jnp.where(causal, _dot_nt(q_dec[:, ks], k_inv[:, ks]), 0.0)
            oi_c.append(_dot(s.astype(BF16), v))
            kv_c.append(_dot_tn(k_end[:, ks], v))
        kvs.append(kv_c)
        o_intras.append(oi_c)
    for n, c0 in enumerate(chunks):
        rows = slice(c0, c0 + c)
        for h in range(GLA_H):
            ks = slice(h * GLA_DK, (h + 1) * GLA_DK)
            vs = slice(h * GLA_DV, (h + 1) * GLA_DV)
            s_prev = state[h]
            o = o_intras[n][h] + _dot(q_decs[n][:, ks], s_prev.astype(BF16))
            state[h] = decays[n][ks, :] * s_prev + kvs[n][h]
            o = o * lax.rsqrt(jnp.mean(o * o, axis=-1, keepdims=True) + LN_EPS) * g_ref[...]
            rg = r_ref[rows, vs]
            o_ref[rows, vs] = o * (rg * _sigmoid(rg))


def _branch_b(z, zlr, wa2_pad_bf, gla_ba, gla_norm_g, batch, seq, tt=256):
    nt = seq // tt
    hk = GLA_H * GLA_DK
    return pl.pallas_call(
        functools.partial(_gla_kernel, tt=tt),
        grid=(batch, nt),
        in_specs=[pl.BlockSpec((tt, hk), lambda b, i: (b * nt + i, COL_B_Q // hk)),
                  pl.BlockSpec((tt, hk), lambda b, i: (b * nt + i, COL_B_K // hk)),
                  pl.BlockSpec((tt, MIX_W), lambda b, i: (b * nt + i, COL_B_V // MIX_W)),
                  pl.BlockSpec((tt, MIX_W), lambda b, i: (b * nt + i, COL_B_R // MIX_W)),
                  pl.BlockSpec((tt, LANES), lambda b, i: (b * nt + i, 0)),
                  pl.BlockSpec((LANES, hk), lambda b, i: (0, 0)),
                  pl.BlockSpec((1, hk), lambda b, i: (0, 0)),
                  pl.BlockSpec((1, GLA_DV), lambda b, i: (0, 0))],
        out_specs=pl.BlockSpec((tt, MIX_W), lambda b, i: (b * nt + i, 0)),
        out_shape=jax.ShapeDtypeStruct((batch * seq, MIX_W), F32),
        scratch_shapes=[pltpu.VMEM((GLA_H, GLA_DK, GLA_DV), F32)],
        compiler_params=_cparams("arbitrary", "arbitrary"),
        name="branch_b_gla",
    )(z, z, z, z, zlr, wa2_pad_bf, gla_ba.reshape(1, -1), gla_norm_g.reshape(1, -1))


SB_PAIRS = 4
SB_QROWS = 256
SB_DEAD_LOG = -104.0


def _sb_kernel(q_ref, k_ref, v_ref, uj_ref, o_ref, kb, vb, ls_s, lk_s, w_s, carry_s, acc_s):
    qi = pl.program_id(2)
    blk = SB_BLOCK
    qr = SB_QROWS
    rows = pl.ds(pl.multiple_of(qi * qr, qr), qr)
    kb[rows, :] = k_ref[...].astype(BF16)
    vb[rows, :] = v_ref[...].astype(BF16)

    lane = lax.broadcasted_iota(jnp.int32, (qr, LANES), 1)
    row2 = lax.broadcasted_iota(jnp.int32, (2 * qr, blk), 0)
    col2 = lax.broadcasted_iota(jnp.int32, (2 * qr, blk), 1)
    ahead = jnp.bitwise_and(row2, qr - 1) - col2
    scale = SB_DH ** -0.5
    qs = []
    for p in range(SB_PAIRS):
        q = q_ref[:, p * LANES:(p + 1) * LANES] * scale
        qs.append(jnp.concatenate([jnp.where(lane < SB_DH, q, 0.0).astype(BF16),
                                   jnp.where(lane >= SB_DH, q, 0.0).astype(BF16)], axis=0))
        carry_s[p] = jnp.zeros((2 * qr, LANES), F32)
        acc_s[p] = jnp.zeros((2 * qr, LANES), F32)

    def tile_rows(j):
        return pl.ds(pl.multiple_of(j * blk, blk), blk)

    def stage_a(j, key_shift):
        for p in range(SB_PAIRS):
            ks = kb[tile_rows(j), p * LANES:(p + 1) * LANES]
            zz = _dot_nt(qs[p], ks)
            neg = -zz
            lk = jnp.minimum(neg, 0.0) - jnp.log(1.0 + jnp.exp(jnp.minimum(zz, neg)))
            ls = zz + lk
            if key_shift is not None:
                before = ahead > key_shift
                lk = jnp.where(before, lk, 0.0)
                ls = jnp.where(before, ls, -1e30)
            hi, lo = _split_bf16(lk)
            ls_s[p] = ls
            lk_s[p] = jnp.concatenate([hi, lo], axis=1)

    def stage_b():
        for p in range(SB_PAIRS):
            r = _dot(lk_s[p], uj_ref[...])
            carry = carry_s[p]
            w_s[p] = jnp.exp(ls_s[p] + carry + r[:, :LANES]).astype(BF16)
            carry_s[p] = carry + r[:, LANES:]

    def stage_c(j):
        for p in range(SB_PAIRS):
            acc_s[p] = acc_s[p] + _dot(w_s[p], vb[tile_rows(j), p * LANES:(p + 1) * LANES])

    tiles_per_q = qr // blk
    last = tiles_per_q * qi + tiles_per_q - 1
    stage_a(last, (tiles_per_q - 1) * blk)
    stage_b()
    stage_a(last - 1, (tiles_per_q - 2) * blk)

    def body(state):
        m, _ = state
        for mm in (m, m + 1):
            stage_c(last - mm + 2)
            stage_b()
            stage_a(last - mm, None)
        top = carry_s[0]
        for p in range(1, SB_PAIRS):
            top = jnp.maximum(top, carry_s[p])
        dead = (jnp.max(top) < SB_DEAD_LOG).astype(jnp.int32)
        return m + 2, dead

    m, dead = lax.while_loop(lambda st: jnp.logical_and(st[0] <= last, st[1] == 0), body,
                             (jnp.int32(2), jnp.int32(0)))
    stage_c(last - m + 2)

    @pl.when(dead == 0)
    def _():
        stage_b()
        stage_c(0)

    for p in range(SB_PAIRS):
        acc = acc_s[p]
        o_ref[:, p * LANES:(p + 1) * LANES] = jnp.where(lane < SB_DH, acc[:qr], acc[qr:])


def _branch_c(z, batch, seq):
    assert SB_QROWS == 2 * SB_BLOCK
    nq = seq // SB_QROWS
    width = SB_PAIRS * LANES
    groups = SB_H * SB_DH // width
    blk = SB_BLOCK
    row = lax.broadcasted_iota(jnp.int32, (blk, blk), 0)
    col = lax.broadcasted_iota(jnp.int32, (blk, blk), 1)
    u = jnp.where(row > col, 1.0, 0.0)
    uj = jnp.concatenate([u, jnp.ones((blk, blk), F32)], axis=1)
    uj = jnp.concatenate([uj, uj], axis=0).astype(BF16)
    blockspec = lambda col0: pl.BlockSpec((SB_QROWS, width), lambda b, g, i: (b * nq + i, col0 // width + g))
    stage = lambda lanes, dt: pltpu.VMEM((SB_PAIRS, 2 * SB_QROWS, lanes), dt)
    return pl.pallas_call(
        _sb_kernel,
        grid=(batch, groups, nq),
        in_specs=[blockspec(COL_C_Q), blockspec(COL_C_K), blockspec(COL_C_V),
                  pl.BlockSpec((2 * blk, 2 * blk), lambda b, g, i: (0, 0))],
        out_specs=pl.BlockSpec((SB_QROWS, width), lambda b, g, i: (b * nq + i, g)),
        out_shape=jax.ShapeDtypeStruct((batch * seq, MIX_W), F32),
        scratch_shapes=[pltpu.VMEM((seq, width), BF16), pltpu.VMEM((seq, width), BF16),
                        stage(LANES, F32), stage(2 * LANES, BF16), stage(LANES, BF16),
                        stage(LANES, F32), stage(LANES, F32)],
        compiler_params=_cparams("arbitrary", "arbitrary", "arbitrary"),
        name="branch_c_stick_breaking",
    )(z, z, z, uj)


D_HALO = 8


def _lru_kernel(x_ref, gate_ref, cw_ref, cb_ref, wa_ref, ba_ref, wx_ref, bx_ref, lam_ref, o_ref,
                xbuf, abuf, ubuf, hbuf, hprev, *, tt):
    i = pl.program_id(1)

    @pl.when(i == 0)
    def _():
        xbuf[0:D_HALO, :] = jnp.zeros((D_HALO, MIX_W), F32)
        hprev[...] = jnp.zeros_like(hprev)

    @pl.when(i > 0)
    def _():
        xbuf[0:D_HALO, :] = xbuf[tt:tt + D_HALO, :]

    xbuf[D_HALO:D_HALO + tt, :] = x_ref[...]
    off = D_HALO - (LRU_CONV - 1)
    xc = jnp.zeros((tt, MIX_W), F32) + cb_ref[...]
    for j in range(LRU_CONV):
        xc = xc + cw_ref[j:j + 1, :] * xbuf[off + j:off + j + tt, :]
    xc_bf = xc.astype(BF16)
    r = _sigmoid(_dot(xc_bf, wa_ref[...]) + ba_ref[...])
    gi = _sigmoid(_dot(xc_bf, wx_ref[...]) + bx_ref[...])
    log_a = LRU_C * r * (-_softplus(-lam_ref[...]))
    a = jnp.exp(log_a)
    abuf[...] = a
    ubuf[...] = jnp.sqrt(-jnp.tanh(log_a) * (a * a + 1.0)) * (gi * xc)

    def step(t, h):
        h = abuf[pl.ds(t, 1), :] * h + ubuf[pl.ds(t, 1), :]
        hbuf[pl.ds(t, 1), :] = h
        return h

    hprev[...] = lax.fori_loop(0, tt, step, hprev[...], unroll=8)
    g = gate_ref[...]
    gelu = 0.5 * g * (1.0 + jnp.tanh(0.7978845608028654 * (g + 0.044715 * g * g * g)))
    o_ref[...] = hbuf[...] * gelu


def _branch_d(z, conv_w, conv_b, wa_bd_bf, ba, wx_bd_bf, bx, lam, batch, seq, tt=256):
    nt = seq // tt
    vec = lambda: pl.BlockSpec((1, MIX_W), lambda b, i: (0, 0))
    mat = lambda: pl.BlockSpec((MIX_W, MIX_W), lambda b, i: (0, 0))
    return pl.pallas_call(
        functools.partial(_lru_kernel, tt=tt),
        grid=(batch, nt),
        in_specs=[pl.BlockSpec((tt, MIX_W), lambda b, i: (b * nt + i, COL_D_X // MIX_W)),
                  pl.BlockSpec((tt, MIX_W), lambda b, i: (b * nt + i, COL_D_G // MIX_W)),
                  pl.BlockSpec((LRU_CONV, MIX_W), lambda b, i: (0, 0)),
                  vec(), mat(), vec(), mat(), vec(), vec()],
        out_specs=pl.BlockSpec((tt, MIX_W), lambda b, i: (b * nt + i, 0)),
        out_shape=jax.ShapeDtypeStruct((batch * seq, MIX_W), F32),
        scratch_shapes=[pltpu.VMEM((D_HALO + tt, MIX_W), F32), pltpu.VMEM((tt, MIX_W), F32),
                        pltpu.VMEM((tt, MIX_W), F32), pltpu.VMEM((tt, MIX_W), F32),
                        pltpu.VMEM((1, MIX_W), F32)],
        compiler_params=_cparams("arbitrary", "arbitrary"),
        name="branch_d_rglru",
    )(z, z, conv_w, conv_b.reshape(1, -1), wa_bd_bf, ba.reshape(1, -1), wx_bd_bf, bx.reshape(1, -1),
      lam.reshape(1, -1))


def _merge_kernel(ya_ref, yb_ref, yc_ref, yd_ref, g0_ref, g1_ref, g2_ref, g3_ref, h_ref, wb_ref, wo_ref,
                  bo_ref, lg_ref, lb_ref, o_ref, *, alpha):
    merged = None
    for n, (y_ref, g_ref) in enumerate(((ya_ref, g0_ref), (yb_ref, g1_ref), (yc_ref, g2_ref), (yd_ref, g3_ref))):
        term = _sigmoid(g_ref[...]) * _dot(y_ref[...].astype(BF16), wb_ref[n])
        merged = term if merged is None else merged + term
    mix = _dot(merged.astype(BF16), wo_ref[...]) + bo_ref[...]
    o_ref[...] = _ln(alpha * h_ref[...] + mix, lg_ref[...], lb_ref[...])


def _merge(ya, yb, yc, yd, z, h, w_branch_bf, w_out_bf, b_out, ln_g, ln_b, alpha, tm=256):
    t, d = h.shape
    ysp = lambda: pl.BlockSpec((tm, MIX_W), lambda i: (i, 0))
    gsp = lambda n: pl.BlockSpec((tm, d), lambda i: (i, COL_G // d + n))
    vec = lambda: pl.BlockSpec((1, d), lambda i: (0, 0))
    return pl.pallas_call(
        functools.partial(_merge_kernel, alpha=alpha),
        grid=(t // tm,),
        in_specs=[ysp(), ysp(), ysp(), ysp(), gsp(0), gsp(1), gsp(2), gsp(3),
                  pl.BlockSpec((tm, d), lambda i: (i, 0)),
                  pl.BlockSpec((4, MIX_W, d), lambda i: (0, 0, 0)),
                  pl.BlockSpec((d, d), lambda i: (0, 0)),
                  vec(), vec(), vec()],
        out_specs=pl.BlockSpec((tm, d), lambda i: (i, 0)),
        out_shape=jax.ShapeDtypeStruct((t, d), F32),
        compiler_params=_cparams("arbitrary"),
        name="merge_out_ln",
    )(ya, yb, yc, yd, z, z, z, z, h, w_branch_bf, w_out_bf, b_out.reshape(1, d), ln_g.reshape(1, d),
      ln_b.reshape(1, d))


def _xattn_kernel(h_ref, wq_ref, kv_ref, wo_ref, lg_ref, lb_ref, o_ref, obuf, *, alpha, d):
    h = h_ref[...]
    q = _dot(h.astype(BF16), wq_ref[...])
    dh = d // MEM_H
    scale = dh ** -0.5
    for hd in range(MEM_H):
        cs = slice(hd * dh, (hd + 1) * dh)
        k = kv_ref[:, cs].astype(BF16)
        v = kv_ref[:, d + hd * dh:d + (hd + 1) * dh].astype(BF16)
        s = _dot_nt(q[:, cs].astype(BF16), k) * scale
        s = s - jnp.max(s, axis=-1, keepdims=True)
        p = jnp.exp(s)
        p = p / jnp.sum(p, axis=-1, keepdims=True)
        obuf[:, cs] = _dot(p.astype(BF16), v)
    ca = _dot(obuf[...].astype(BF16), wo_ref[...])
    o_ref[...] = _ln(alpha * h + ca, lg_ref[...], lb_ref[...])


def _xattn(h, kv, wq_bf, wo_bf, ln_g, ln_b, alpha, seq, n_mem, tm=256):
    t, d = h.shape
    per_batch = seq // tm
    vec = lambda: pl.BlockSpec((1, d), lambda i: (0, 0))
    return pl.pallas_call(
        functools.partial(_xattn_kernel, alpha=alpha, d=d),
        grid=(t // tm,),
        in_specs=[pl.BlockSpec((tm, d), lambda i: (i, 0)),
                  pl.BlockSpec((d, d), lambda i: (0, 0)),
                  pl.BlockSpec((n_mem, 2 * d), lambda i: (i // per_batch, 0)),
                  pl.BlockSpec((d, d), lambda i: (0, 0)),
                  vec(), vec()],
        out_specs=pl.BlockSpec((tm, d), lambda i: (i, 0)),
        out_shape=jax.ShapeDtypeStruct((t, d), F32),
        scratch_shapes=[pltpu.VMEM((tm, d), F32)],
        compiler_params=_cparams("arbitrary"),
        name="xattn_ln",
    )(h, wq_bf, kv, wo_bf, ln_g.reshape(1, d), ln_b.reshape(1, d))


ROUTE_IDX, ROUTE_GATE, ROUTE_RANK = 0, 4, 8


def _router_kernel(h_ref, whi_ref, wlo_ref, b_ref, lt_ref, route_ref, cnt_ref, cnt):
    @pl.when(pl.program_id(0) == 0)
    def _():
        cnt[...] = jnp.zeros_like(cnt)

    x_hi, x_lo = _split_bf16(h_ref[...])
    logits = _dot(x_hi, whi_ref[...]) + _dot(x_lo, whi_ref[...]) + _dot(x_hi, wlo_ref[...]) + b_ref[...]
    tm = logits.shape[0]
    lane = lax.broadcasted_iota(jnp.int32, (tm, LANES), 1).astype(F32)
    cur = logits
    vals, idxs = [], []
    for _ in range(TOP_K):
        m = jnp.max(cur, axis=-1, keepdims=True)
        idx = jnp.min(jnp.where(cur == m, lane, float(LANES)), axis=-1, keepdims=True)
        vals.append(m)
        idxs.append(idx)
        cur = jnp.where(lane == idx, -jnp.inf, cur)
    ex = [jnp.exp(v - vals[0]) for v in vals]
    denom = ex[0] + ex[1] + ex[2] + ex[3]
    onehot = jnp.zeros((tm, LANES), F32)
    for idx in idxs:
        onehot = onehot + jnp.where(lane == idx, 1.0, 0.0)
    before = _dot(lt_ref[...], onehot.astype(BF16)) + cnt[...]
    route = jnp.zeros((tm, LANES), F32)
    for k in range(TOP_K):
        rank = jnp.sum(jnp.where(lane == idxs[k], before, 0.0), axis=-1, keepdims=True)
        route = jnp.where(lane == float(ROUTE_IDX + k), idxs[k], route)
        route = jnp.where(lane == float(ROUTE_GATE + k), ex[k] / denom, route)
        route = jnp.where(lane == float(ROUTE_RANK + k), rank, route)
    route_ref[...] = route
    cnt[...] = cnt[...] + jnp.sum(onehot, axis=0, keepdims=True)
    cnt_ref[...] = cnt[...]


def _router(h, router_w, router_b, tm=256):
    t, d = h.shape
    w_pad = jnp.zeros((d, LANES), F32).at[:, :N_EXPERTS].set(router_w)
    w_hi = w_pad.astype(BF16)
    w_lo = (w_pad - w_hi.astype(F32)).astype(BF16)
    b_pad = jnp.full((1, LANES), -1e30, F32).at[0, :N_EXPERTS].set(router_b)
    row = lax.broadcasted_iota(jnp.int32, (tm, tm), 0)
    col = lax.broadcasted_iota(jnp.int32, (tm, tm), 1)
    lower = jnp.where(col < row, 1.0, 0.0).astype(BF16)
    return pl.pallas_call(
        _router_kernel,
        grid=(t // tm,),
        in_specs=[pl.BlockSpec((tm, d), lambda i: (i, 0)),
                  pl.BlockSpec((d, LANES), lambda i: (0, 0)),
                  pl.BlockSpec((d, LANES), lambda i: (0, 0)),
                  pl.BlockSpec((1, LANES), lambda i: (0, 0)),
                  pl.BlockSpec((tm, tm), lambda i: (0, 0))],
        out_specs=[pl.BlockSpec((tm, LANES), lambda i: (i, 0)),
                   pl.BlockSpec((1, LANES), lambda i: (0, 0))],
        out_shape=[jax.ShapeDtypeStruct((t, LANES), F32), jax.ShapeDtypeStruct((1, LANES), F32)],
        scratch_shapes=[pltpu.VMEM((1, LANES), F32)],
        compiler_params=_cparams("arbitrary"),
        name="moe_router",
    )(h, w_hi, w_lo, b_pad, lower)


EXPERT_BUFS = 3
ROW_TILES = 8


def _expert_kernel(be_ref, nu_ref, tok_ref, x_hbm, w1_ref, b1_ref, w2_ref, b2_ref, o_ref, xbuf, w1b, w2b, sems, *, ff):
    i = pl.program_id(0)
    n_used = nu_ref[0]

    def row_copy(blk, slot, r):
        return pltpu.make_async_copy(x_hbm.at[pl.ds(tok_ref[blk * MOE_BLOCK + r], 1)],
                                     xbuf.at[slot, pl.ds(r, 1)], sems.at[slot])

    def issue(blk, slot):
        def body(r, _):
            row_copy(blk, slot, r).start()
            return 0
        lax.fori_loop(0, MOE_BLOCK, body, 0, unroll=8)

    def drain(slot):
        pltpu.make_async_copy(x_hbm.at[pl.ds(0, MOE_BLOCK)], xbuf.at[slot], sems.at[slot]).wait()

    def ffn(slot, prefetch):
        x = xbuf[slot].astype(BF16)
        if prefetch:
            ahead = lax.rem(i + 2, EXPERT_BUFS)
            for r in range(MOE_BLOCK):
                row_copy(i + 2, ahead, r).start()
        hcat = _dot(x, w1b[...]) + b1_ref[0]
        g = jnp.minimum(hcat[:, :ff], SWIGLU_LIMIT)
        lin = jnp.clip(hcat[:, ff:], -SWIGLU_LIMIT, SWIGLU_LIMIT)
        act = g * _sigmoid(SWIGLU_ALPHA * g) * (lin + 1.0)
        out = _dot(act.astype(BF16), w2b[...]) + b2_ref[0]
        for j in range(ROW_TILES):
            o_ref[pl.ds(j, MOE_BLOCK, stride=ROW_TILES), :] = out[:, j * LANES:(j + 1) * LANES]

    @pl.when(i < n_used)
    def _():
        slot = lax.rem(i, EXPERT_BUFS)

        @pl.when(i == 0)
        def _():
            issue(0, 0)

            @pl.when(n_used > 1)
            def _():
                issue(1, 1)

        @pl.when(jnp.logical_or(i == 0, be_ref[i] != be_ref[jnp.maximum(i - 1, 0)]))
        def _():
            w1b[...] = w1_ref[0, 0].astype(BF16)
            w2b[...] = w2_ref[0, 0].astype(BF16)

        drain(slot)

        @pl.when(i + 2 < n_used)
        def _():
            ffn(slot, True)

        @pl.when(i + 2 >= n_used)
        def _():
            ffn(slot, False)

    @pl.when(i >= n_used)
    def _():
        o_ref[...] = jnp.zeros_like(o_ref)


def _experts(h, slot_tok, block_e, n_used, w1, b1, w2, b2, layer):
    t, d = h.shape
    ff = w2.shape[2]
    n_slots = slot_tok.shape[0]
    n_blocks = n_slots // MOE_BLOCK
    return pl.pallas_call(
        functools.partial(_expert_kernel, ff=ff),
        grid_spec=pltpu.PrefetchScalarGridSpec(
            num_scalar_prefetch=3,
            grid=(n_blocks,),
            in_specs=[pl.BlockSpec(memory_space=pl.ANY),
                      pl.BlockSpec((1, 1, d, 2 * ff), lambda i, be, nu, tok: (layer, be[i], 0, 0)),
                      pl.BlockSpec((1, 1, 2 * ff), lambda i, be, nu, tok: (be[i], 0, 0)),
                      pl.BlockSpec((1, 1, ff, d), lambda i, be, nu, tok: (layer, be[i], 0, 0)),
                      pl.BlockSpec((1, 1, d), lambda i, be, nu, tok: (be[i], 0, 0))],
            out_specs=pl.BlockSpec((MOE_BLOCK * ROW_TILES, LANES), lambda i, be, nu, tok: (i, 0)),
            scratch_shapes=[pltpu.VMEM((EXPERT_BUFS, MOE_BLOCK, d), F32), pltpu.VMEM((d, 2 * ff), BF16),
                            pltpu.VMEM((ff, d), BF16), pltpu.SemaphoreType.DMA((EXPERT_BUFS,))],
        ),
        out_shape=jax.ShapeDtypeStruct((n_slots * ROW_TILES, LANES), F32),
        compiler_params=_cparams("arbitrary"),
        name="moe_experts",
    )(block_e, n_used, slot_tok, h, w1, b1.reshape(N_EXPERTS, 1, 2 * ff), w2, b2.reshape(N_EXPERTS, 1, d))


COMBINE_TOKENS = 128


def _combine_kernel(dest_ref, ys_hbm, route_ref, h_ref, lg_ref, lb_ref, o_ref, buf, sems, *, alpha):
    i = pl.program_id(0)
    n = pl.num_programs(0)
    tm = COMBINE_TOKENS

    def row_copy(step, slot, t, k):
        src = pl.multiple_of(dest_ref[(step * tm + t) * TOP_K + k] * ROW_TILES, ROW_TILES)
        return pltpu.make_async_copy(ys_hbm.at[pl.ds(src, ROW_TILES)], buf.at[slot, k, pl.ds(t * ROW_TILES, ROW_TILES)],
                                     sems.at[slot])

    slot = lax.rem(i, 2)

    @pl.when(i == 0)
    def _():
        def body(t, _):
            for k in range(TOP_K):
                row_copy(0, 0, t, k).start()
            return 0
        lax.fori_loop(0, tm, body, 0)

    @pl.when(i + 1 < n)
    def _():
        for t in range(tm):
            for k in range(TOP_K):
                row_copy(i + 1, 1 - slot, t, k).start()

    for k in range(TOP_K):
        pltpu.make_async_copy(ys_hbm.at[pl.ds(0, tm * ROW_TILES)], buf.at[slot, k], sems.at[slot]).wait()
    route = route_ref[...]
    pieces = []
    for j in range(ROW_TILES):
        part = jnp.zeros((tm, LANES), F32)
        for k in range(TOP_K):
            part = part + route[:, ROUTE_GATE + k:ROUTE_GATE + k + 1] * buf[slot, k, pl.ds(j, tm, stride=ROW_TILES), :]
        pieces.append(part)
    ff = jnp.concatenate(pieces, axis=1)
    o_ref[...] = _ln(alpha * h_ref[...] + ff, lg_ref[...], lb_ref[...])


def _combine(ys, dest_flat, route, h, ln_g, ln_b, alpha):
    t, d = h.shape
    tm = COMBINE_TOKENS
    vec = lambda: pl.BlockSpec((1, d), lambda i, dst: (0, 0))
    return pl.pallas_call(
        functools.partial(_combine_kernel, alpha=alpha),
        grid_spec=pltpu.PrefetchScalarGridSpec(
            num_scalar_prefetch=1,
            grid=(t // tm,),
            in_specs=[pl.BlockSpec(memory_space=pl.ANY),
                      pl.BlockSpec((tm, LANES), lambda i, dst: (i, 0)),
                      pl.BlockSpec((tm, d), lambda i, dst: (i, 0)),
                      vec(), vec()],
            out_specs=pl.BlockSpec((tm, d), lambda i, dst: (i, 0)),
            scratch_shapes=[pltpu.VMEM((2, TOP_K, tm * ROW_TILES, LANES), F32), pltpu.SemaphoreType.DMA((2,))],
        ),
        out_shape=jax.ShapeDtypeStruct((t, d), F32),
        compiler_params=_cparams("arbitrary"),
        name="moe_combine_ln",
    )(dest_flat, ys, route, h, ln_g.reshape(1, d), ln_b.reshape(1, d))


def _moe(h, router_w, router_b, w1, b1, w2, b2, ln_g, ln_b, alpha, layer):
    t, d = h.shape
    route, cnt = _router(h, router_w, router_b)
    idx = route[:, ROUTE_IDX:ROUTE_IDX + TOP_K].astype(jnp.int32)
    rank = route[:, ROUTE_RANK:ROUTE_RANK + TOP_K].astype(jnp.int32)
    counts = cnt[0, :N_EXPERTS].astype(jnp.int32)
    padded = ((counts + MOE_BLOCK - 1) // MOE_BLOCK) * MOE_BLOCK
    pend = jnp.cumsum(padded)
    pstart = pend - padded
    dest = (pstart[idx] + rank).reshape(t * TOP_K)
    n_blocks = t * TOP_K // MOE_BLOCK + N_EXPERTS
    n_used = (pend[-1:] // MOE_BLOCK).astype(jnp.int32)
    block_start = jnp.arange(n_blocks, dtype=jnp.int32) * MOE_BLOCK
    block_e = jnp.minimum(jnp.sum((pend[None, :] <= block_start[:, None]).astype(jnp.int32), axis=1), N_EXPERTS - 1)
    tok = jnp.arange(t * TOP_K, dtype=jnp.int32) // TOP_K
    slot_tok = jnp.zeros((n_blocks * MOE_BLOCK,), jnp.int32).at[dest].set(tok, unique_indices=True)
    ys = _experts(h, slot_tok, block_e, n_used, w1, b1, w2, b2, layer)
    return _combine(ys, dest, route, h, ln_g, ln_b, alpha)


def _block_diag(w):
    n, bw, _ = w.shape
    eye = jnp.eye(n, dtype=w.dtype)
    return (eye[:, None, :, None] * w[:, :, None, :]).reshape(n * bw, n * bw)


def kernel(x, mem, ln0_g, ln0_b, w_in, b_in, conv_a_w, conv_a_b, ln_a_g, ln_a_b, gla_wa2, gla_ba, gla_norm_g, conv_d_w, conv_d_b, lru_wa, lru_ba, lru_wx, lru_bx, lru_lambda, w_branch, w_out, b_out, ln1_g, ln1_b, ca_wq, ca_wk, ca_wv, ca_wo, ln2_g, ln2_b, router_w, router_b, moe_w1, moe_b1, moe_w2, moe_b2, ln3_g, ln3_b):
    batch, seq, d = x.shape
    n_mem = mem.shape[1]
    depth = w_in.shape[0]
    alpha = (2 * depth) ** 0.25
    t = batch * seq
    h = _layer_norm(x.reshape(t, d), ln0_g, ln0_b)
    mem2 = mem.reshape(batch * n_mem, d)
    w_main_all = _repack_w_in(w_in)
    w_lr_all = jnp.zeros((depth, d, LANES), F32).at[:, :, :GLA_LOWRANK].set(w_in[:, :, LR_START:LR_END]).astype(BF16)
    w_kv_all = jnp.concatenate([ca_wk, ca_wv], axis=2).astype(BF16)
    for l in range(depth):
        b_main = jnp.concatenate([b_in[l][:LR_START], b_in[l][LR_END:]])
        b_lr = jnp.zeros((LANES,), F32).at[:GLA_LOWRANK].set(b_in[l][LR_START:LR_END])
        z = _matmul_bias(h, w_main_all, b_main, tm=1024, tn=1024, layer=l)
        zlr = _matmul_bias(h, w_lr_all, b_lr, tm=1024, tn=LANES, layer=l)
        wa2_pad = jnp.zeros((LANES, GLA_H * GLA_DK), F32).at[:GLA_LOWRANK].set(gla_wa2[l]).astype(BF16)
        ya = _branch_a(z, conv_a_w[l], conv_a_b[l], ln_a_g[l], ln_a_b[l], batch, seq)
        yb = _branch_b(z, zlr, wa2_pad, gla_ba[l], gla_norm_g[l], batch, seq)
        yc = _branch_c(z, batch, seq)
        yd = _branch_d(z, conv_d_w[l], conv_d_b[l], _block_diag(lru_wa[l]).astype(BF16), lru_ba[l],
                       _block_diag(lru_wx[l]).astype(BF16), lru_bx[l], lru_lambda[l], batch, seq)
        h = _merge(ya, yb, yc, yd, z, h, w_branch[l].astype(BF16), w_out[l].astype(BF16), b_out[l],
                   ln1_g[l], ln1_b[l], alpha)
        kv = _matmul_bias(mem2, w_kv_all, jnp.zeros((2 * d,), F32), tm=batch * n_mem, tn=512, layer=l)
        h = _xattn(h, kv, ca_wq[l].astype(BF16), ca_wo[l].astype(BF16), ln2_g[l], ln2_b[l], alpha, seq, n_mem)
        h = _moe(h, router_w[l], router_b[l], moe_w1, moe_b1[l], moe_w2, moe_b2[l], ln3_g[l], ln3_b[l], alpha, l)
    return h.reshape(batch, seq, d)
```

```python
import functools

import jax
import jax.numpy as jnp
from jax import lax
from jax.experimental import pallas as pl
from jax.experimental.pallas import tpu as pltpu

F32 = jnp.float32
BF16 = jnp.bfloat16

MIX_W = 512
CONV_A_WIDTH = 31
GLA_H, GLA_DK, GLA_DV = 4, 64, 128
GLA_LOWRANK = 16
GLA_TAU = 16.0
GLA_CHUNK = 64
SB_H, SB_DH, SB_BLOCK = 8, 64, 128
LRU_C = 8.0
LRU_CONV = 4
MEM_H = 4
N_EXPERTS, TOP_K = 32, 4
MOE_BLOCK = 256
SWIGLU_LIMIT = 7.0
SWIGLU_ALPHA = 1.702
LN_EPS = 1e-5
LANES = 128
SUBLANES = 8
VMEM_LIMIT = 48 * 1024 * 1024

COL_A_VAL, COL_A_GATE = 0, 512
COL_B_Q, COL_B_K, COL_B_V, COL_B_R = 1024, 1280, 1536, 2048
COL_C_Q, COL_C_K, COL_C_V = 2560, 3072, 3584
COL_D_X, COL_D_G = 4096, 4608
COL_G = 5120
Z_COLS = 9216
LR_START, LR_END = 2560, 2576


def _cparams(*sem):
    return pltpu.CompilerParams(dimension_semantics=sem, vmem_limit_bytes=VMEM_LIMIT)


def _ln(x, g, b):
    mu = jnp.mean(x, axis=-1, keepdims=True)
    xc = x - mu
    var = jnp.mean(xc * xc, axis=-1, keepdims=True)
    return xc * lax.rsqrt(var + LN_EPS) * g + b


def _sigmoid(x):
    return 1.0 / (1.0 + jnp.exp(-x))


def _softplus(x):
    return jnp.maximum(x, 0.0) + jnp.log1p(jnp.exp(-jnp.abs(x)))


def _split_bf16(x):
    hi = x.astype(BF16)
    lo = (x - hi.astype(F32)).astype(BF16)
    return hi, lo


def _dot(a, b):
    return jnp.dot(a, b, preferred_element_type=F32)


def _dot_nt(a, b):
    return lax.dot_general(a, b, (((1,), (1,)), ((), ())), preferred_element_type=F32)


def _dot_tn(a, b):
    return lax.dot_general(a, b, (((0,), (0,)), ((), ())), preferred_element_type=F32)


def _ln_kernel(x_ref, g_ref, b_ref, o_ref):
    o_ref[...] = _ln(x_ref[...], g_ref[...], b_ref[...])


def _layer_norm(x, g, b, tm=512):
    t, d = x.shape
    return pl.pallas_call(
        _ln_kernel,
        grid=(t // tm,),
        in_specs=[pl.BlockSpec((tm, d), lambda i: (i, 0)),
                  pl.BlockSpec((1, d), lambda i: (0, 0)),
                  pl.BlockSpec((1, d), lambda i: (0, 0))],
        out_specs=pl.BlockSpec((tm, d), lambda i: (i, 0)),
        out_shape=jax.ShapeDtypeStruct((t, d), F32),
        compiler_params=_cparams("arbitrary"),
        name="layer_norm",
    )(x, g.reshape(1, d), b.reshape(1, d))


def _mm_kernel(a_ref, w_ref, b_ref, o_ref, abf_ref):
    @pl.when(pl.program_id(1) == 0)
    def _():
        abf_ref[...] = a_ref[...].astype(BF16)

    o_ref[...] = _dot(abf_ref[...], w_ref[0]) + b_ref[...]


def _matmul_bias(a, w_bf, b, tm, tn, layer=0):
    m, k = a.shape
    n = w_bf.shape[2]
    return pl.pallas_call(
        _mm_kernel,
        grid=(m // tm, n // tn),
        in_specs=[pl.BlockSpec((tm, k), lambda i, j: (i, 0)),
                  pl.BlockSpec((1, k, tn), lambda i, j: (layer, 0, j)),
                  pl.BlockSpec((1, tn), lambda i, j: (0, j))],
        out_specs=pl.BlockSpec((tm, tn), lambda i, j: (i, j)),
        out_shape=jax.ShapeDtypeStruct((m, n), F32),
        scratch_shapes=[pltpu.VMEM((tm, k), BF16)],
        compiler_params=_cparams("arbitrary", "arbitrary"),
        name="matmul_bias",
    )(a, w_bf, b.reshape(1, n))


PACK_TILE = 512


def _repack_kernel(a_ref, b_ref, o_ref):
    j = pl.program_id(1)
    lr = LR_END - LR_START

    @pl.when(j < LR_START // PACK_TILE)
    def _():
        o_ref[0] = a_ref[0].astype(BF16)

    @pl.when(j >= LR_START // PACK_TILE)
    def _():
        x = jnp.concatenate([a_ref[0], b_ref[0]], axis=1)
        o_ref[0] = pltpu.roll(x, x.shape[1] - lr, axis=1)[:, :PACK_TILE].astype(BF16)


def _repack_w_in(w_in):
    depth, d, _ = w_in.shape
    assert LR_START % PACK_TILE == 0 and Z_COLS % PACK_TILE == 0
    lr = LR_END - LR_START
    n_tiles = Z_COLS // PACK_TILE
    inner = w_in[:, :, PACK_TILE:Z_COLS].reshape(depth, d, n_tiles - 1, PACK_TILE)[..., :lr]
    tails = jnp.concatenate([inner, w_in[:, :, Z_COLS:Z_COLS + lr][:, :, None, :]], axis=2)
    tails = jnp.pad(tails, ((0, 0), (0, 0), (0, 0), (0, LANES - lr))).reshape(depth, d, n_tiles * LANES)
    return pl.pallas_call(
        _repack_kernel,
        grid=(depth, n_tiles),
        in_specs=[pl.BlockSpec((1, d, PACK_TILE), lambda l, j: (l, 0, j)),
                  pl.BlockSpec((1, d, LANES), lambda l, j: (l, 0, j))],
        out_specs=pl.BlockSpec((1, d, PACK_TILE), lambda l, j: (l, 0, j)),
        out_shape=jax.ShapeDtypeStruct((depth, d, Z_COLS), BF16),
        compiler_params=_cparams("arbitrary", "arbitrary"),
        name="repack_w_in",
    )(w_in, tails)


A_HALO = 32
A_ROWS = 64


def _conv_a_kernel(val_ref, gate_ref, w_ref, cb_ref, g_ref, b_ref, o_ref, ubuf, ush, ybuf, *, tt):
    i = pl.program_id(1)

    @pl.when(i == 0)
    def _():
        ubuf[0:A_HALO, :] = jnp.zeros((A_HALO, MIX_W), F32)

    @pl.when(i > 0)
    def _():
        ubuf[0:A_HALO, :] = ubuf[tt:tt + A_HALO, :]

    ubuf[A_HALO:A_HALO + tt, :] = val_ref[...] * _sigmoid(gate_ref[...])
    span = tt + A_HALO - SUBLANES
    for s in range(1, SUBLANES):
        ush[s - 1] = ubuf[s:s + span, :]
    off = A_HALO - (CONV_A_WIDTH - 1)
    for r0 in range(0, tt, A_ROWS):
        for c0 in range(0, MIX_W, LANES):
            acc = jnp.zeros((A_ROWS, LANES), F32) + cb_ref[:, c0:c0 + LANES]
            for j in range(CONV_A_WIDTH):
                s, a = (off + j) % SUBLANES, (off + j) // SUBLANES * SUBLANES
                if s == 0:
                    tap = ubuf[r0 + a:r0 + a + A_ROWS, c0:c0 + LANES]
                else:
                    tap = ush[s - 1, r0 + a:r0 + a + A_ROWS, c0:c0 + LANES]
                acc = acc + w_ref[j:j + 1, c0:c0 + LANES] * tap
            ybuf[r0:r0 + A_ROWS, c0:c0 + LANES] = acc
    y = _ln(ybuf[...], g_ref[...], b_ref[...])
    o_ref[...] = y * _sigmoid(y)


def _branch_a(z, conv_w, conv_b, ln_g, ln_b, batch, seq, tt=256):
    nt = seq // tt
    vec = lambda: pl.BlockSpec((1, MIX_W), lambda b, i: (0, 0))
    return pl.pallas_call(
        functools.partial(_conv_a_kernel, tt=tt),
        grid=(batch, nt),
        in_specs=[pl.BlockSpec((tt, MIX_W), lambda b, i: (b * nt + i, COL_A_VAL // MIX_W)),
                  pl.BlockSpec((tt, MIX_W), lambda b, i: (b * nt + i, COL_A_GATE // MIX_W)),
                  pl.BlockSpec((CONV_A_WIDTH, MIX_W), lambda b, i: (0, 0)),
                  vec(), vec(), vec()],
        out_specs=pl.BlockSpec((tt, MIX_W), lambda b, i: (b * nt + i, 0)),
        out_shape=jax.ShapeDtypeStruct((batch * seq, MIX_W), F32),
        scratch_shapes=[pltpu.VMEM((A_HALO + tt, MIX_W), F32),
                        pltpu.VMEM((SUBLANES - 1, tt + A_HALO - SUBLANES, MIX_W), F32),
                        pltpu.VMEM((tt, MIX_W), F32)],
        compiler_params=_cparams("arbitrary", "arbitrary"),
        name="branch_a_conv",
    )(z, z, conv_w, conv_b.reshape(1, -1), ln_g.reshape(1, -1), ln_b.reshape(1, -1))


def _gla_kernel(q_ref, k_ref, v_ref, r_ref, lr_ref, wa2_ref, ba_ref, g_ref, o_ref, state, *, tt):
    @pl.when(pl.program_id(1) == 0)
    def _():
        state[...] = jnp.zeros_like(state)

    c = GLA_CHUNK
    row = lax.broadcasted_iota(jnp.int32, (c, c), 0)
    col = lax.broadcasted_iota(jnp.int32, (c, c), 1)
    causal = col <= row
    tri = jnp.where(causal, 1.0, 0.0).astype(BF16)
    ones = jnp.ones((c, LANES), BF16)
    scale = GLA_DK ** -0.5
    chunks = range(0, tt, c)
    q_decs, kvs, decays, o_intras = [], [], [], []
    for c0 in chunks:
        rows = slice(c0, c0 + c)
        x = _dot(lr_ref[rows, :].astype(BF16), wa2_ref[...]) + ba_ref[...]
        la = -_softplus(-x) / GLA_TAU
        la_hi, la_lo = _split_bf16(la)
        b = _dot(tri, la_hi) + _dot(tri, la_lo)
        b_last = b[c - 1:c, :]
        q_dec = ((q_ref[rows, :] * scale) * jnp.exp(b)).astype(BF16)
        k = k_ref[rows, :]
        k_inv = (k * jnp.exp(-b)).astype(BF16)
        k_end = (k * jnp.exp(b_last - b)).astype(BF16)
        dsum = _dot_tn(la_hi, ones) + _dot_tn(la_lo, ones)
        q_decs.append(q_dec)
        decays.append(jnp.exp(dsum))
        kv_c, oi_c = [], []
        for h in range(GLA_H):
            ks = slice(h * GLA_DK, (h + 1) * GLA_DK)
            v = v_ref[rows, h * GLA_DV:(h + 1) * GLA_DV].astype(BF16)
            s = jnp.where(causal, _dot_nt(q_dec[:, ks], k_inv[:, ks]), 0.0)
            oi_c.append(_dot(s.astype(BF16), v))
            kv_c.append(_dot_tn(k_end[:, ks], v))
        kvs.append(kv_c)
        o_intras.append(oi_c)
    for n, c0 in enumerate(chunks):
        rows = slice(c0, c0 + c)
        for h in range(GLA_H):
            ks = slice(h * GLA_DK, (h + 1) * GLA_DK)
            vs = slice(h * GLA_DV, (h + 1) * GLA_DV)
            s_prev = state[h]
            o = o_intras[n][h] + _dot(q_decs[n][:, ks], s_prev.astype(BF16))
            state[h] = decays[n][ks, :] * s_prev + kvs[n][h]
            o = o * lax.rsqrt(jnp.mean(o * o, axis=-1, keepdims=True) + LN_EPS) * g_ref[...]
            rg = r_ref[rows, vs]
            o_ref[rows, vs] = o * (rg * _sigmoid(rg))


def _branch_b(z, zlr, wa2_pad_bf, gla_ba, gla_norm_g, batch, seq, tt=256):
    nt = seq // tt
    hk = GLA_H * GLA_DK
    return pl.pallas_call(
        functools.partial(_gla_kernel, tt=tt),
        grid=(batch, nt),
        in_specs=[pl.BlockSpec((tt, hk), lambda b, i: (b * nt + i, COL_B_Q // hk)),
                  pl.BlockSpec((tt, hk), lambda b, i: (b * nt + i, COL_B_K // hk)),
                  pl.BlockSpec((tt, MIX_W), lambda b, i: (b * nt + i, COL_B_V // MIX_W)),
                  pl.BlockSpec((tt, MIX_W), lambda b, i: (b * nt + i, COL_B_R // MIX_W)),
                  pl.BlockSpec((tt, LANES), lambda b, i: (b * nt + i, 0)),
                  pl.BlockSpec((LANES, hk), lambda b, i: (0, 0)),
                  pl.BlockSpec((1, hk), lambda b, i: (0, 0)),
                  pl.BlockSpec((1, GLA_DV), lambda b, i: (0, 0))],
        out_specs=pl.BlockSpec((tt, MIX_W), lambda b, i: (b * nt + i, 0)),
        out_shape=jax.ShapeDtypeStruct((batch * seq, MIX_W), F32),
        scratch_shapes=[pltpu.VMEM((GLA_H, GLA_DK, GLA_DV), F32)],
        compiler_params=_cparams("arbitrary", "arbitrary"),
        name="branch_b_gla",
    )(z, z, z, z, zlr, wa2_pad_bf, gla_ba.reshape(1, -1), gla_norm_g.reshape(1, -1))


SB_PAIRS = 4
SB_QROWS = 256
SB_DEAD_LOG = -104.0


def _sb_kernel(q_ref, k_ref, v_ref, uj_ref, o_ref, kb, vb, ls_s, lk_s, w_s, carry_s, acc_s):
    qi = pl.program_id(2)
    blk = SB_BLOCK
    qr = SB_QROWS
    rows = pl.ds(pl.multiple_of(qi * qr, qr), qr)
    kb[rows, :] = k_ref[...].astype(BF16)
    vb[rows, :] = v_ref[...].astype(BF16)

    lane = lax.broadcasted_iota(jnp.int32, (qr, LANES), 1)
    row2 = lax.broadcasted_iota(jnp.int32, (2 * qr, blk), 0)
    col2 = lax.broadcasted_iota(jnp.int32, (2 * qr, blk), 1)
    ahead = jnp.bitwise_and(row2, qr - 1) - col2
    scale = SB_DH ** -0.5
    qs = []
    for p in range(SB_PAIRS):
        q = q_ref[:, p * LANES:(p + 1) * LANES] * scale
        qs.append(jnp.concatenate([jnp.where(lane < SB_DH, q, 0.0).astype(BF16),
                                   jnp.where(lane >= SB_DH, q, 0.0).astype(BF16)], axis=0))
        carry_s[p] = jnp.zeros((2 * qr, LANES), F32)
        acc_s[p] = jnp.zeros((2 * qr, LANES), F32)

    def tile_rows(j):
        return pl.ds(pl.multiple_of(j * blk, blk), blk)

    def stage_a(j, key_shift):
        for p in range(SB_PAIRS):
            ks = kb[tile_rows(j), p * LANES:(p + 1) * LANES]
            zz = _dot_nt(qs[p], ks)
            neg = -zz
            lk = jnp.minimum(neg, 0.0) - jnp.log(1.0 + jnp.exp(jnp.minimum(zz, neg)))
            ls = zz + lk
            if key_shift is not None:
                before = ahead > key_shift
                lk = jnp.where(before, lk, 0.0)
                ls = jnp.where(before, ls, -1e30)
            hi, lo = _split_bf16(lk)
            ls_s[p] = ls
            lk_s[p] = jnp.concatenate([hi, lo], axis=1)

    def stage_b():
        for p in range(SB_PAIRS):
            r = _dot(lk_s[p], uj_ref[...])
            carry = carry_s[p]
            w_s[p] = jnp.exp(ls_s[p] + carry + r[:, :LANES]).astype(BF16)
            carry_s[p] = carry + r[:, LANES:]

    def stage_c(j):
        for p in range(SB_PAIRS):
            acc_s[p] = acc_s[p] + _dot(w_s[p], vb[tile_rows(j), p * LANES:(p + 1) * LANES])

    tiles_per_q = qr // blk
    last = tiles_per_q * qi + tiles_per_q - 1
    stage_a(last, (tiles_per_q - 1) * blk)
    stage_b()
    stage_a(last - 1, (tiles_per_q - 2) * blk)

    def body(state):
        m, _ = state
        for mm in (m, m + 1):
            stage_c(last - mm + 2)
            stage_b()
            stage_a(last - mm, None)
        top = carry_s[0]
        for p in range(1, SB_PAIRS):
            top = jnp.maximum(top, carry_s[p])
        dead = (jnp.max(top) < SB_DEAD_LOG).astype(jnp.int32)
        return m + 2, dead

    m, dead = lax.while_loop(lambda st: jnp.logical_and(st[0] <= last, st[1] == 0), body,
                             (jnp.int32(2), jnp.int32(0)))
    stage_c(last - m + 2)

    @pl.when(dead == 0)
    def _():
        stage_b()
        stage_c(0)

    for p in range(SB_PAIRS):
        acc = acc_s[p]
        o_ref[:, p * LANES:(p + 1) * LANES] = jnp.where(lane < SB_DH, acc[:qr], acc[qr:])


def _branch_c(z, batch, seq):
    assert SB_QROWS == 2 * SB_BLOCK
    nq = seq // SB_QROWS
    width = SB_PAIRS * LANES
    groups = SB_H * SB_DH // width
    blk = SB_BLOCK
    row = lax.broadcasted_iota(jnp.int32, (blk, blk), 0)
    col = lax.broadcasted_iota(jnp.int32, (blk, blk), 1)
    u = jnp.where(row > col, 1.0, 0.0)
    uj = jnp.concatenate([u, jnp.ones((blk, blk), F32)], axis=1)
    uj = jnp.concatenate([uj, uj], axis=0).astype(BF16)
    blockspec = lambda col0: pl.BlockSpec((SB_QROWS, width), lambda b, g, i: (b * nq + i, col0 // width + g))
    stage = lambda lanes, dt: pltpu.VMEM((SB_PAIRS, 2 * SB_QROWS, lanes), dt)
    return pl.pallas_call(
        _sb_kernel,
        grid=(batch, groups, nq),
        in_specs=[blockspec(COL_C_Q), blockspec(COL_C_K), blockspec(COL_C_V),
                  pl.BlockSpec((2 * blk, 2 * blk), lambda b, g, i: (0, 0))],
        out_specs=pl.BlockSpec((SB_QROWS, width), lambda b, g, i: (b * nq + i, g)),
        out_shape=jax.ShapeDtypeStruct((batch * seq, MIX_W), F32),
        scratch_shapes=[pltpu.VMEM((seq, width), BF16), pltpu.VMEM((seq, width), BF16),
                        stage(LANES, F32), stage(2 * LANES, BF16), stage(LANES, BF16),
                        stage(LANES, F32), stage(LANES, F32)],
        compiler_params=_cparams("arbitrary", "arbitrary", "arbitrary"),
        name="branch_c_stick_breaking",
    )(z, z, z, uj)


D_HALO = 8


def _lru_kernel(x_ref, gate_ref, cw_ref, cb_ref, wa_ref, ba_ref, wx_ref, bx_ref, lam_ref, o_ref,
                xbuf, abuf, ubuf, hbuf, hprev, *, tt):
    i = pl.program_id(1)

    @pl.when(i == 0)
    def _():
        xbuf[0:D_HALO, :] = jnp.zeros((D_HALO, MIX_W), F32)
        hprev[...] = jnp.zeros_like(hprev)

    @pl.when(i > 0)
    def _():
        xbuf[0:D_HALO, :] = xbuf[tt:tt + D_HALO, :]

    xbuf[D_HALO:D_HALO + tt, :] = x_ref[...]
    off = D_HALO - (LRU_CONV - 1)
    xc = jnp.zeros((tt, MIX_W), F32) + cb_ref[...]
    for j in range(LRU_CONV):
        xc = xc + cw_ref[j:j + 1, :] * xbuf[off + j:off + j + tt, :]
    xc_bf = xc.astype(BF16)
    r = _sigmoid(_dot(xc_bf, wa_ref[...]) + ba_ref[...])
    gi = _sigmoid(_dot(xc_bf, wx_ref[...]) + bx_ref[...])
    log_a = LRU_C * r * (-_softplus(-lam_ref[...]))
    a = jnp.exp(log_a)
    abuf[...] = a
    ubuf[...] = jnp.sqrt(-jnp.tanh(log_a) * (a * a + 1.0)) * (gi * xc)

    def step(t, h):
        h = abuf[pl.ds(t, 1), :] * h + ubuf[pl.ds(t, 1), :]
        hbuf[pl.ds(t, 1), :] = h
        return h

    hprev[...] = lax.fori_loop(0, tt, step, hprev[...], unroll=8)
    g = gate_ref[...]
    gelu = 0.5 * g * (1.0 + jnp.tanh(0.7978845608028654 * (g + 0.044715 * g * g * g)))
    o_ref[...] = hbuf[...] * gelu


def _branch_d(z, conv_w, conv_b, wa_bd_bf, ba, wx_bd_bf, bx, lam, batch, seq, tt=256):
    nt = seq // tt
    vec = lambda: pl.BlockSpec((1, MIX_W), lambda b, i: (0, 0))
    mat = lambda: pl.BlockSpec((MIX_W, MIX_W), lambda b, i: (0, 0))
    return pl.pallas_call(
        functools.partial(_lru_kernel, tt=tt),
        grid=(batch, nt),
        in_specs=[pl.BlockSpec((tt, MIX_W), lambda b, i: (b * nt + i, COL_D_X // MIX_W)),
                  pl.BlockSpec((tt, MIX_W), lambda b, i: (b * nt + i, COL_D_G // MIX_W)),
                  pl.BlockSpec((LRU_CONV, MIX_W), lambda b, i: (0, 0)),
                  vec(), mat(), vec(), mat(), vec(), vec()],
        out_specs=pl.BlockSpec((tt, MIX_W), lambda b, i: (b * nt + i, 0)),
        out_shape=jax.ShapeDtypeStruct((batch * seq, MIX_W), F32),
        scratch_shapes=[pltpu.VMEM((D_HALO + tt, MIX_W), F32), pltpu.VMEM((tt, MIX_W), F32),
                        pltpu.VMEM((tt, MIX_W), F32), pltpu.VMEM((tt, MIX_W), F32),
                        pltpu.VMEM((1, MIX_W), F32)],
        compiler_params=_cparams("arbitrary", "arbitrary"),
        name="branch_d_rglru",
    )(z, z, conv_w, conv_b.reshape(1, -1), wa_bd_bf, ba.reshape(1, -1), wx_bd_bf, bx.reshape(1, -1),
      lam.reshape(1, -1))


def _merge_kernel(ya_ref, yb_ref, yc_ref, yd_ref, g0_ref, g1_ref, g2_ref, g3_ref, h_ref, wb_ref, wo_ref,
                  bo_ref, lg_ref, lb_ref, o_ref, *, alpha):
    merged = None
    for n, (y_ref, g_ref) in enumerate(((ya_ref, g0_ref), (yb_ref, g1_ref), (yc_ref, g2_ref), (yd_ref, g3_ref))):
        term = _sigmoid(g_ref[...]) * _dot(y_ref[...].astype(BF16), wb_ref[n])
        merged = term if merged is None else merged + term
    mix = _dot(merged.astype(BF16), wo_ref[...]) + bo_ref[...]
    o_ref[...] = _ln(alpha * h_ref[...] + mix, lg_ref[...], lb_ref[...])


def _merge(ya, yb, yc, yd, z, h, w_branch_bf, w_out_bf, b_out, ln_g, ln_b, alpha, tm=256):
    t, d = h.shape
    ysp = lambda: pl.BlockSpec((tm, MIX_W), lambda i: (i, 0))
    gsp = lambda n: pl.BlockSpec((tm, d), lambda i: (i, COL_G // d + n))
    vec = lambda: pl.BlockSpec((1, d), lambda i: (0, 0))
    return pl.pallas_call(
        functools.partial(_merge_kernel, alpha=alpha),
        grid=(t // tm,),
        in_specs=[ysp(), ysp(), ysp(), ysp(), gsp(0), gsp(1), gsp(2), gsp(3),
                  pl.BlockSpec((tm, d), lambda i: (i, 0)),
                  pl.BlockSpec((4, MIX_W, d), lambda i: (0, 0, 0)),
                  pl.BlockSpec((d, d), lambda i: (0, 0)),
                  vec(), vec(), vec()],
        out_specs=pl.BlockSpec((tm, d), lambda i: (i, 0)),
        out_shape=jax.ShapeDtypeStruct((t, d), F32),
        compiler_params=_cparams("arbitrary"),
        name="merge_out_ln",
    )(ya, yb, yc, yd, z, z, z, z, h, w_branch_bf, w_out_bf, b_out.reshape(1, d), ln_g.reshape(1, d),
      ln_b.reshape(1, d))


def _xattn_kernel(h_ref, wq_ref, kv_ref, wo_ref, lg_ref, lb_ref, o_ref, obuf, *, alpha, d):
    h = h_ref[...]
    q = _dot(h.astype(BF16), wq_ref[...])
    dh = d // MEM_H
    scale = dh ** -0.5
    for hd in range(MEM_H):
        cs = slice(hd * dh, (hd + 1) * dh)
        k = kv_ref[:, cs].astype(BF16)
        v = kv_ref[:, d + hd * dh:d + (hd + 1) * dh].astype(BF16)
        s = _dot_nt(q[:, cs].astype(BF16), k) * scale
        s = s - jnp.max(s, axis=-1, keepdims=True)
        p = jnp.exp(s)
        p = p / jnp.sum(p, axis=-1, keepdims=True)
        obuf[:, cs] = _dot(p.astype(BF16), v)
    ca = _dot(obuf[...].astype(BF16), wo_ref[...])
    o_ref[...] = _ln(alpha * h + ca, lg_ref[...], lb_ref[...])


def _xattn(h, kv, wq_bf, wo_bf, ln_g, ln_b, alpha, seq, n_mem, tm=256):
    t, d = h.shape
    per_batch = seq // tm
    vec = lambda: pl.BlockSpec((1, d), lambda i: (0, 0))
    return pl.pallas_call(
        functools.partial(_xattn_kernel, alpha=alpha, d=d),
        grid=(t // tm,),
        in_specs=[pl.BlockSpec((tm, d), lambda i: (i, 0)),
                  pl.BlockSpec((d, d), lambda i: (0, 0)),
                  pl.BlockSpec((n_mem, 2 * d), lambda i: (i // per_batch, 0)),
                  pl.BlockSpec((d, d), lambda i: (0, 0)),
                  vec(), vec()],
        out_specs=pl.BlockSpec((tm, d), lambda i: (i, 0)),
        out_shape=jax.ShapeDtypeStruct((t, d), F32),
        scratch_shapes=[pltpu.VMEM((tm, d), F32)],
        compiler_params=_cparams("arbitrary"),
        name="xattn_ln",
    )(h, wq_bf, kv, wo_bf, ln_g.reshape(1, d), ln_b.reshape(1, d))


ROUTE_IDX, ROUTE_GATE, ROUTE_RANK = 0, 4, 8


def _router_kernel(h_ref, whi_ref, wlo_ref, b_ref, lt_ref, route_ref, cnt_ref, cnt):
    @pl.when(pl.program_id(0) == 0)
    def _():
        cnt[...] = jnp.zeros_like(cnt)

    x_hi, x_lo = _split_bf16(h_ref[...])
    logits = _dot(x_hi, whi_ref[...]) + _dot(x_lo, whi_ref[...]) + _dot(x_hi, wlo_ref[...]) + b_ref[...]
    tm = logits.shape[0]
    lane = lax.broadcasted_iota(jnp.int32, (tm, LANES), 1).astype(F32)
    cur = logits
    vals, idxs = [], []
    for _ in range(TOP_K):
        m = jnp.max(cur, axis=-1, keepdims=True)
        idx = jnp.min(jnp.where(cur == m, lane, float(LANES)), axis=-1, keepdims=True)
        vals.append(m)
        idxs.append(idx)
        cur = jnp.where(lane == idx, -jnp.inf, cur)
    ex = [jnp.exp(v - vals[0]) for v in vals]
    denom = ex[0] + ex[1] + ex[2] + ex[3]
    onehot = jnp.zeros((tm, LANES), F32)
    for idx in idxs:
        onehot = onehot + jnp.where(lane == idx, 1.0, 0.0)
    before = _dot(lt_ref[...], onehot.astype(BF16)) + cnt[...]
    route = jnp.zeros((tm, LANES), F32)
    for k in range(TOP_K):
        rank = jnp.sum(jnp.where(lane == idxs[k], before, 0.0), axis=-1, keepdims=True)
        route = jnp.where(lane == float(ROUTE_IDX + k), idxs[k], route)
        route = jnp.where(lane == float(ROUTE_GATE + k), ex[k] / denom, route)
        route = jnp.where(lane == float(ROUTE_RANK + k), rank, route)
    route_ref[...] = route
    cnt[...] = cnt[...] + jnp.sum(onehot, axis=0, keepdims=True)
    cnt_ref[...] = cnt[...]


def _router(h, router_w, router_b, tm=256):
    t, d = h.shape
    w_pad = jnp.zeros((d, LANES), F32).at[:, :N_EXPERTS].set(router_w)
    w_hi = w_pad.astype(BF16)
    w_lo = (w_pad - w_hi.astype(F32)).astype(BF16)
    b_pad = jnp.full((1, LANES), -1e30, F32).at[0, :N_EXPERTS].set(router_b)
    row = lax.broadcasted_iota(jnp.int32, (tm, tm), 0)
    col = lax.broadcasted_iota(jnp.int32, (tm, tm), 1)
    lower = jnp.where(col < row, 1.0, 0.0).astype(BF16)
    return pl.pallas_call(
        _router_kernel,
        grid=(t // tm,),
        in_specs=[pl.BlockSpec((tm, d), lambda i: (i, 0)),
                  pl.BlockSpec((d, LANES), lambda i: (0, 0)),
                  pl.BlockSpec((d, LANES), lambda i: (0, 0)),
                  pl.BlockSpec((1, LANES), lambda i: (0, 0)),
                  pl.BlockSpec((tm, tm), lambda i: (0, 0))],
        out_specs=[pl.BlockSpec((tm, LANES), lambda i: (i, 0)),
                   pl.BlockSpec((1, LANES), lambda i: (0, 0))],
        out_shape=[jax.ShapeDtypeStruct((t, LANES), F32), jax.ShapeDtypeStruct((1, LANES), F32)],
        scratch_shapes=[pltpu.VMEM((1, LANES), F32)],
        compiler_params=_cparams("arbitrary"),
        name="moe_router",
    )(h, w_hi, w_lo, b_pad, lower)


EXPERT_BUFS = 3
ROW_TILES = 8


def _expert_kernel(be_ref, nu_ref, tok_ref, x_hbm, w1_ref, b1_ref, w2_ref, b2_ref, o_ref, xbuf, w1b, w2b, sems, *, ff):
    i = pl.program_id(0)
    n_used = nu_ref[0]

    def row_copy(blk, slot, r):
        return pltpu.make_async_copy(x_hbm.at[pl.ds(tok_ref[blk * MOE_BLOCK + r], 1)],
                                     xbuf.at[slot, pl.ds(r, 1)], sems.at[slot])

    def issue(blk, slot):
        def body(r, _):
            row_copy(blk, slot, r).start()
            return 0
        lax.fori_loop(0, MOE_BLOCK, body, 0, unroll=8)

    def drain(slot):
        pltpu.make_async_copy(x_hbm.at[pl.ds(0, MOE_BLOCK)], xbuf.at[slot], sems.at[slot]).wait()

    def ffn(slot, prefetch):
        x = xbuf[slot].astype(BF16)
        if prefetch:
            ahead = lax.rem(i + 2, EXPERT_BUFS)
            for r in range(MOE_BLOCK):
                row_copy(i + 2, ahead, r).start()
        hcat = _dot(x, w1b[...]) + b1_ref[0]
        g = jnp.minimum(hcat[:, :ff], SWIGLU_LIMIT)
        lin = jnp.clip(hcat[:, ff:], -SWIGLU_LIMIT, SWIGLU_LIMIT)
        act = g * _sigmoid(SWIGLU_ALPHA * g) * (lin + 1.0)
        out = _dot(act.astype(BF16), w2b[...]) + b2_ref[0]
        for j in range(ROW_TILES):
            o_ref[pl.ds(j, MOE_BLOCK, stride=ROW_TILES), :] = out[:, j * LANES:(j + 1) * LANES]

    @pl.when(i < n_used)
    def _():
        slot = lax.rem(i, EXPERT_BUFS)

        @pl.when(i == 0)
        def _():
            issue(0, 0)

            @pl.when(n_used > 1)
            def _():
                issue(1, 1)

        @pl.when(jnp.logical_or(i == 0, be_ref[i] != be_ref[jnp.maximum(i - 1, 0)]))
        def _():
            w1b[...] = w1_ref[0, 0].astype(BF16)
            w2b[...] = w2_ref[0, 0].astype(BF16)

        drain(slot)

        @pl.when(i + 2 < n_used)
        def _():
            ffn(slot, True)

        @pl.when(i + 2 >= n_used)
        def _():
            ffn(slot, False)

    @pl.when(i >= n_used)
    def _():
        o_ref[...] = jnp.zeros_like(o_ref)


def _experts(h, slot_tok, block_e, n_used, w1, b1, w2, b2, layer):
    t, d = h.shape
    ff = w2.shape[2]
    n_slots = slot_tok.shape[0]
    n_blocks = n_slots // MOE_BLOCK
    return pl.pallas_call(
        functools.partial(_expert_kernel, ff=ff),
        grid_spec=pltpu.PrefetchScalarGridSpec(
            num_scalar_prefetch=3,
            grid=(n_blocks,),
            in_specs=[pl.BlockSpec(memory_space=pl.ANY),
                      pl.BlockSpec((1, 1, d, 2 * ff), lambda i, be, nu, tok: (layer, be[i], 0, 0)),
                      pl.BlockSpec((1, 1, 2 * ff), lambda i, be, nu, tok: (be[i], 0, 0)),
                      pl.BlockSpec((1, 1, ff, d), lambda i, be, nu, tok: (layer, be[i], 0, 0)),
                      pl.BlockSpec((1, 1, d), lambda i, be, nu, tok: (be[i], 0, 0))],
            out_specs=pl.BlockSpec((MOE_BLOCK * ROW_TILES, LANES), lambda i, be, nu, tok: (i, 0)),
            scratch_shapes=[pltpu.VMEM((EXPERT_BUFS, MOE_BLOCK, d), F32), pltpu.VMEM((d, 2 * ff), BF16),
                            pltpu.VMEM((ff, d), BF16), pltpu.SemaphoreType.DMA((EXPERT_BUFS,))],
        ),
        out_shape=jax.ShapeDtypeStruct((n_slots * ROW_TILES, LANES), F32),
        compiler_params=_cparams("arbitrary"),
        name="moe_experts",
    )(block_e, n_used, slot_tok, h, w1, b1.reshape(N_EXPERTS, 1, 2 * ff), w2, b2.reshape(N_EXPERTS, 1, d))


COMBINE_TOKENS = 128


def _combine_kernel(dest_ref, ys_hbm, route_ref, h_ref, lg_ref, lb_ref, o_ref, buf, sems, *, alpha):
    i = pl.program_id(0)
    n = pl.num_programs(0)
    tm = COMBINE_TOKENS

    def row_copy(step, slot, t, k):
        src = pl.multiple_of(dest_ref[(step * tm + t) * TOP_K + k] * ROW_TILES, ROW_TILES)
        return pltpu.make_async_copy(ys_hbm.at[pl.ds(src, ROW_TILES)], buf.at[slot, k, pl.ds(t * ROW_TILES, ROW_TILES)],
                                     sems.at[slot])

    slot = lax.rem(i, 2)

    @pl.when(i == 0)
    def _():
        def body(t, _):
            for k in range(TOP_K):
                row_copy(0, 0, t, k).start()
            return 0
        lax.fori_loop(0, tm, body, 0)

    @pl.when(i + 1 < n)
    def _():
        for t in range(tm):
            for k in range(TOP_K):
                row_copy(i + 1, 1 - slot, t, k).start()

    for k in range(TOP_K):
        pltpu.make_async_copy(ys_hbm.at[pl.ds(0, tm * ROW_TILES)], buf.at[slot, k], sems.at[slot]).wait()
    route = route_ref[...]
    pieces = []
    for j in range(ROW_TILES):
        part = jnp.zeros((tm, LANES), F32)
        for k in range(TOP_K):
            part = part + route[:, ROUTE_GATE + k:ROUTE_GATE + k + 1] * buf[slot, k, pl.ds(j, tm, stride=ROW_TILES), :]
        pieces.append(part)
    ff = jnp.concatenate(pieces, axis=1)
    o_ref[...] = _ln(alpha * h_ref[...] + ff, lg_ref[...], lb_ref[...])


def _combine(ys, dest_flat, route, h, ln_g, ln_b, alpha):
    t, d = h.shape
    tm = COMBINE_TOKENS
    vec = lambda: pl.BlockSpec((1, d), lambda i, dst: (0, 0))
    return pl.pallas_call(
        functools.partial(_combine_kernel, alpha=alpha),
        grid_spec=pltpu.PrefetchScalarGridSpec(
            num_scalar_prefetch=1,
            grid=(t // tm,),
            in_specs=[pl.BlockSpec(memory_space=pl.ANY),
                      pl.BlockSpec((tm, LANES), lambda i, dst: (i, 0)),
                      pl.BlockSpec((tm, d), lambda i, dst: (i, 0)),
                      vec(), vec()],
            out_specs=pl.BlockSpec((tm, d), lambda i, dst: (i, 0)),
            scratch_shapes=[pltpu.VMEM((2, TOP_K, tm * ROW_TILES, LANES), F32), pltpu.SemaphoreType.DMA((2,))],
        ),
        out_shape=jax.ShapeDtypeStruct((t, d), F32),
        compiler_params=_cparams("arbitrary"),
        name="moe_combine_ln",
    )(dest_flat, ys, route, h, ln_g.reshape(1, d), ln_b.reshape(1, d))


def _moe(h, router_w, router_b, w1, b1, w2, b2, ln_g, ln_b, alpha, layer):
    t, d = h.shape
    route, cnt = _router(h, router_w, router_b)
    idx = route[:, ROUTE_IDX:ROUTE_IDX + TOP_K].astype(jnp.int32)
    rank = route[:, ROUTE_RANK:ROUTE_RANK + TOP_K].astype(jnp.int32)
    counts = cnt[0, :N_EXPERTS].astype(jnp.int32)
    padded = ((counts + MOE_BLOCK - 1) // MOE_BLOCK) * MOE_BLOCK
    pend = jnp.cumsum(padded)
    pstart = pend - padded
    dest = (pstart[idx] + rank).reshape(t * TOP_K)
    n_blocks = t * TOP_K // MOE_BLOCK + N_EXPERTS
    n_used = (pend[-1:] // MOE_BLOCK).astype(jnp.int32)
    block_start = jnp.arange(n_blocks, dtype=jnp.int32) * MOE_BLOCK
    block_e = jnp.minimum(jnp.sum((pend[None, :] <= block_start[:, None]).astype(jnp.int32), axis=1), N_EXPERTS - 1)
    tok = jnp.arange(t * TOP_K, dtype=jnp.int32) // TOP_K
    slot_tok = jnp.zeros((n_blocks * MOE_BLOCK,), jnp.int32).at[dest].set(tok, unique_indices=True)
    ys = _experts(h, slot_tok, block_e, n_used, w1, b1, w2, b2, layer)
    return _combine(ys, dest, route, h, ln_g, ln_b, alpha)


def _block_diag(w):
    n, bw, _ = w.shape
    eye = jnp.eye(n, dtype=w.dtype)
    return (eye[:, None, :, None] * w[:, :, None, :]).reshape(n * bw, n * bw)


def kernel(x, mem, ln0_g, ln0_b, w_in, b_in, conv_a_w, conv_a_b, ln_a_g, ln_a_b, gla_wa2, gla_ba, gla_norm_g, conv_d_w, conv_d_b, lru_wa, lru_ba, lru_wx, lru_bx, lru_lambda, w_branch, w_out, b_out, ln1_g, ln1_b, ca_wq, ca_wk, ca_wv, ca_wo, ln2_g, ln2_b, router_w, router_b, moe_w1, moe_b1, moe_w2, moe_b2, ln3_g, ln3_b):
    batch, seq, d = x.shape
    n_mem = mem.shape[1]
    depth = w_in.shape[0]
    alpha = (2 * depth) ** 0.25
    t = batch * seq
    h = _layer_norm(x.reshape(t, d), ln0_g, ln0_b)
    mem2 = mem.reshape(batch * n_mem, d)
    w_main_all = _repack_w_in(w_in)
    w_lr_all = jnp.zeros((depth, d, LANES), F32).at[:, :, :GLA_LOWRANK].set(w_in[:, :, LR_START:LR_END]).astype(BF16)
    w_kv_all = jnp.concatenate([ca_wk, ca_wv], axis=2).astype(BF16)
    for l in range(depth):
        b_main = jnp.concatenate([b_in[l][:LR_START], b_in[l][LR_END:]])
        b_lr = jnp.zeros((LANES,), F32).at[:GLA_LOWRANK].set(b_in[l][LR_START:LR_END])
        z = _matmul_bias(h, w_main_all, b_main, tm=1024, tn=1024, layer=l)
        zlr = _matmul_bias(h, w_lr_all, b_lr, tm=1024, tn=LANES, layer=l)
        wa2_pad = jnp.zeros((LANES, GLA_H * GLA_DK), F32).at[:GLA_LOWRANK].set(gla_wa2[l]).astype(BF16)
        ya = _branch_a(z, conv_a_w[l], conv_a_b[l], ln_a_g[l], ln_a_b[l], batch, seq)
        yb = _branch_b(z, zlr, wa2_pad, gla_ba[l], gla_norm_g[l], batch, seq)
        yc = _branch_c(z, batch, seq)
        yd = _branch_d(z, conv_d_w[l], conv_d_b[l], _block_diag(lru_wa[l]).astype(BF16), lru_ba[l],
                       _block_diag(lru_wx[l]).astype(BF16), lru_bx[l], lru_lambda[l], batch, seq)
        h = _merge(ya, yb, yc, yd, z, h, w_branch[l].astype(BF16), w_out[l].astype(BF16), b_out[l],
                   ln1_g[l], ln1_b[l], alpha)
        kv = _matmul_bias(mem2, w_kv_all, jnp.zeros((2 * d,), F32), tm=batch * n_mem, tn=512, layer=l)
        h = _xattn(h, kv, ca_wq[l].astype(BF16), ca_wo[l].astype(BF16), ln2_g[l], ln2_b[l], alpha, seq, n_mem)
        h = _moe(h, router_w[l], router_b[l], moe_w1, moe_b1[l], moe_w2, moe_b2[l], ln3_g[l], ln3_b[l], alpha, l)
    return h.reshape(batch, seq, d)
```
